```python
import math
import jax, jax.numpy as jnp
from jax import lax
import numpy as np

D_MODEL = 1024
BATCH = 4
SEQ = 4096
DEPTH = 2

PLE_DIM = 256
ATTN_HEAD_DIM = 64
ATTN_WIDTH = D_MODEL // 2
ATTN_HEADS = ATTN_WIDTH // ATTN_HEAD_DIM
RET_WIDTH = D_MODEL - ATTN_WIDTH
RET_HEADS = 4
RET_HEAD_DIM = RET_WIDTH // RET_HEADS
MIX_WIDTH = ATTN_WIDTH + RET_WIDTH
IN_WIDTH = 3 * ATTN_WIDTH + 4 * RET_WIDTH
SPLITS = [ATTN_WIDTH, 2 * ATTN_WIDTH, 3 * ATTN_WIDTH,
          3 * ATTN_WIDTH + RET_WIDTH, 3 * ATTN_WIDTH + 2 * RET_WIDTH,
          3 * ATTN_WIDTH + 3 * RET_WIDTH]
MOBA_BLOCK = 256
MOBA_TOPK = 3
MOBA_QCHUNK = 32
RET_CHUNK = 256
ROPE_BASE = 10000.0
D_FF = -(-8 * D_MODEL // (3 * 256)) * 256
EPS = 1e-6

kernel_name = "hymba_moba_retnet_ple_trunk"


def rms_norm(x, g):
    xf = x.astype(jnp.float32)
    y = xf * lax.rsqrt(jnp.mean(xf * xf, axis=-1, keepdims=True) + EPS)
    return (y * g.astype(jnp.float32)).astype(x.dtype)


def split_heads(t, n_heads):
    b, s, w = t.shape
    return t.reshape(b, s, n_heads, w // n_heads).transpose(0, 2, 1, 3)


def merge_heads(t):
    b, h, s, d = t.shape
    return t.transpose(0, 2, 1, 3).reshape(b, s, h * d)


def moba_attention(q, k, v):
    B, H, S, d = q.shape
    nb = S // MOBA_BLOCK
    scale = d ** -0.5
    kb = k.reshape(B, H, nb, MOBA_BLOCK, d)
    vb = v.reshape(B, H, nb, MOBA_BLOCK, d)
    k_mean = jnp.mean(kb.astype(jnp.float32), axis=3)
    gate = jnp.einsum('bhsd,bhnd->bhsn', q.astype(jnp.float32), k_mean)
    q_blk = jnp.arange(S) // MOBA_BLOCK
    past = jnp.arange(nb)[None, :] < q_blk[:, None]
    gate = jnp.where(past, gate, -jnp.inf)
    _, top_idx = lax.top_k(gate, MOBA_TOPK)
    bi = jnp.arange(B)[:, None, None, None]
    hi = jnp.arange(H)[None, :, None, None]
    n_chunks = S // MOBA_QCHUNK

    def chunk_fn(c):
        start = c * MOBA_QCHUNK
        blk = start // MOBA_BLOCK
        qc = lax.dynamic_slice_in_dim(q, start, MOBA_QCHUNK, axis=2)
        idx = lax.dynamic_slice_in_dim(top_idx, start, MOBA_QCHUNK, axis=2)
        valid = idx < blk
        kg = kb[bi, hi, idx]
        vg = vb[bi, hi, idx]
        s_sel = jnp.einsum('bhqd,bhqnkd->bhqnk', qc, kg).astype(jnp.float32) * scale
        s_sel = jnp.where(valid[..., None], s_sel, -jnp.inf)
        k_own = lax.dynamic_index_in_dim(kb, blk, axis=2, keepdims=False)
        v_own = lax.dynamic_index_in_dim(vb, blk, axis=2, keepdims=False)
        s_own = jnp.einsum('bhqd,bhkd->bhqk', qc, k_own).astype(jnp.float32) * scale
        qpos = start % MOBA_BLOCK + jnp.arange(MOBA_QCHUNK)
        kpos = jnp.arange(MOBA_BLOCK)
        s_own = jnp.where(kpos[None, :] <= qpos[:, None], s_own, -jnp.inf)
        logits = jnp.concatenate(
            [s_sel.reshape(B, H, MOBA_QCHUNK, MOBA_TOPK * MOBA_BLOCK), s_own], axis=-1)
        probs = jax.nn.softmax(logits, axis=-1)
        p_sel = probs[..., :MOBA_TOPK * MOBA_BLOCK].reshape(
            B, H, MOBA_QCHUNK, MOBA_TOPK, MOBA_BLOCK).astype(v.dtype)
        p_own = probs[..., MOBA_TOPK * MOBA_BLOCK:].astype(v.dtype)
        return (jnp.einsum('bhqnk,bhqnkd->bhqd', p_sel, vg)
                + jnp.einsum('bhqk,bhkd->bhqd', p_own, v_own))

    outs = lax.map(chunk_fn, jnp.arange(n_chunks))
    return outs.transpose(1, 2, 0, 3, 4).reshape(B, H, S, d)


def rotary(x, pos):
    d = x.shape[-1]
    inv = 1.0 / (ROPE_BASE ** jnp.linspace(0.0, 1.0, d // 2, dtype=jnp.float32))
    ang = pos[:, None].astype(jnp.float32) * inv[None, :]
    sin, cos = jnp.sin(ang), jnp.cos(ang)
    x1, x2 = x[..., 0::2], x[..., 1::2]
    out = jnp.stack([x1 * cos - x2 * sin, x1 * sin + x2 * cos], axis=-1)
    return out.reshape(x.shape)


def retention(q, k, v):
    B, H, S, dk = q.shape
    dv = v.shape[-1]
    C = RET_CHUNK
    nc = S // C
    log_g = jnp.log1p(-jnp.exp2(-5.0 - jnp.arange(H, dtype=jnp.float32)))
    qc = q.reshape(B, H, nc, C, dk)
    kc = k.reshape(B, H, nc, C, dk)
    vc = v.reshape(B, H, nc, C, dv)
    i = jnp.arange(C, dtype=jnp.float32)
    rel = i[:, None] - i[None, :]
    dmask = jnp.where(rel[None] >= 0,
                      jnp.exp(jnp.maximum(rel, 0.0)[None] * log_g[:, None, None]), 0.0)
    scores = jnp.einsum('bhnid,bhnjd->bhnij', qc, kc) * dmask[None, :, None]
    y_inner = jnp.einsum('bhnij,bhnjv->bhniv', scores, vc)
    k_dec = jnp.exp((C - 1 - i)[None, :] * log_g[:, None])
    kv = jnp.einsum('bhnjd,bhnjv->bhndv', kc * k_dec[None, :, None, :, None], vc)
    g_chunk = jnp.exp(C * log_g)[None, :, None, None]

    def step(state, kv_n):
        return state * g_chunk + kv_n, state

    _, states = lax.scan(step, jnp.zeros((B, H, dk, dv), jnp.float32),
                         kv.transpose(2, 0, 1, 3, 4))
    states = states.transpose(1, 2, 0, 3, 4)
    q_dec = jnp.exp((i + 1.0)[None, :] * log_g[:, None])
    y_cross = jnp.einsum('bhnid,bhndv->bhniv', qc * q_dec[None, :, None, :, None], states)
    return (y_inner + y_cross).reshape(B, H, S, dv)


def hybrid_layer(h, p_i, attn_norm_g, w_in, ret_norm_g, w_out, ffn_norm_g,
                 w_ffn_in, w_ffn_out, ple_norm_g, w_ple_gate, w_ple_proj):
    B, S, _ = h.shape
    s_pad = max(-(-S // MOBA_BLOCK) * MOBA_BLOCK, (MOBA_TOPK + 1) * MOBA_BLOCK)
    u = rms_norm(h, attn_norm_g) @ w_in
    u = jnp.pad(u, ((0, 0), (0, s_pad - S), (0, 0)))
    aq, ak, av, rq, rk, rv, rg = jnp.split(u, SPLITS, axis=-1)
    a = moba_attention(split_heads(aq, ATTN_HEADS), split_heads(ak, ATTN_HEADS),
                       split_heads(av, ATTN_HEADS))
    a = merge_heads(a)
    pos = jnp.arange(s_pad)
    rqh = rotary(split_heads(rq, RET_HEADS).astype(jnp.float32), pos)
    rkh = rotary(split_heads(rk, RET_HEADS).astype(jnp.float32), pos) * (RET_HEAD_DIM ** -0.5)
    r = retention(rqh, rkh, split_heads(rv, RET_HEADS).astype(jnp.float32))
    r = r * lax.rsqrt(jnp.mean(r * r, axis=-1, keepdims=True) + EPS)
    r = r * ret_norm_g.astype(jnp.float32).reshape(1, RET_HEADS, 1, RET_HEAD_DIM)
    r = (jax.nn.silu(rg.astype(jnp.float32)) * merge_heads(r)).astype(h.dtype)
    mix = jnp.concatenate([a, r], axis=-1)[:, :S]
    h = h + mix @ w_out
    z = rms_norm(h, ffn_norm_g) @ w_ffn_in
    zg, zu = jnp.split(z, [D_FF], axis=-1)
    h = h + (jax.nn.silu(zg) * zu) @ w_ffn_out
    gate = jax.nn.sigmoid(rms_norm(h, ple_norm_g) @ w_ple_gate)
    h = h + gate * (p_i @ w_ple_proj)
    return h


def setup_inputs(seed: int = 0) -> dict:
    key = jax.random.key(seed)
    ks = jax.random.split(key, 16)
    f32 = jnp.float32

    def w(k, shape, fan_in):
        return jax.random.normal(k, shape, f32) * (fan_in ** -0.5)

    def gain(k, shape):
        return 1.0 + 0.05 * jax.random.normal(k, shape, f32)

    return {
        "x": jax.random.normal(ks[0], (BATCH, SEQ, D_MODEL), f32),
        "p": jax.random.normal(ks[1], (DEPTH, BATCH, SEQ, PLE_DIM), f32),
        "attn_norm_g": gain(ks[2], (DEPTH, D_MODEL)),
        "w_in": w(ks[3], (DEPTH, D_MODEL, IN_WIDTH), D_MODEL),
        "ret_norm_g": gain(ks[4], (DEPTH, RET_WIDTH)),
        "w_out": w(ks[5], (DEPTH, MIX_WIDTH, D_MODEL), MIX_WIDTH),
        "ffn_norm_g": gain(ks[6], (DEPTH, D_MODEL)),
        "w_ffn_in": w(ks[7], (DEPTH, D_MODEL, 2 * D_FF), D_MODEL),
        "w_ffn_out": w(ks[8], (DEPTH, D_FF, D_MODEL), D_FF),
        "ple_norm_g": gain(ks[9], (DEPTH, D_MODEL)),
        "w_ple_gate": w(ks[10], (DEPTH, D_MODEL, D_MODEL), D_MODEL),
        "w_ple_proj": w(ks[11], (DEPTH, PLE_DIM, D_MODEL), PLE_DIM),
        "final_norm_g": gain(ks[12], (D_MODEL,)),
    }


def reference(x, p, attn_norm_g, w_in, ret_norm_g, w_out, ffn_norm_g, w_ffn_in,
              w_ffn_out, ple_norm_g, w_ple_gate, w_ple_proj, final_norm_g):
    h = x
    for i in range(DEPTH):
        h = hybrid_layer(h, p[i], attn_norm_g[i], w_in[i], ret_norm_g[i], w_out[i],
                         ffn_norm_g[i], w_ffn_in[i], w_ffn_out[i], ple_norm_g[i],
                         w_ple_gate[i], w_ple_proj[i])
    return rms_norm(h, final_norm_g)
```

```python
import functools

import jax
import jax.numpy as jnp
from jax import lax
from jax.experimental import pallas as pl
from jax.experimental.pallas import tpu as pltpu

D_MODEL = 1024
PLE_DIM = 256
ATTN_WIDTH = 512
ATTN_HEAD_DIM = 64
RET_WIDTH = 512
RET_HEADS = 4
RET_HEAD_DIM = 128
IN_WIDTH = 3 * ATTN_WIDTH + 4 * RET_WIDTH
MOBA_BLOCK = 256
MOBA_TOPK = 3
RET_CHUNK = 256
ROPE_BASE = 10000.0
D_FF = 2816
EPS = 1e-6

LANES = 128
FF_CHUNK = 256
N_FF_CHUNKS = D_FF // FF_CHUNK
ROW_TILE = 512
IN_COL_CHUNK = 512
MASKED = -1e30
VMEM_LIMIT = 52 * 1024 * 1024

F32 = jnp.float32
BF16 = jnp.bfloat16


def _dot(a, b):
    return jnp.dot(a, b, preferred_element_type=F32)


def _dot_nt(a, b):
    return lax.dot_general(a, b, (((1,), (1,)), ((), ())), preferred_element_type=F32)


def _rms(x, g):
    return x * lax.rsqrt(jnp.mean(x * x, axis=-1, keepdims=True) + EPS) * g


def _sigmoid(x):
    return 1.0 / (1.0 + jnp.exp(-x))


def _resident(shape):
    zeros = (0,) * len(shape)
    return pl.BlockSpec(shape, lambda *_: zeros, pipeline_mode=pl.Buffered(1))


def _params(*sem):
    return pltpu.CompilerParams(dimension_semantics=sem, vmem_limit_bytes=VMEM_LIMIT)


def _in_proj_kernel(x_ref, g_ref, w_ref, o_ref):
    n = _rms(x_ref[...], g_ref[...]).astype(BF16)
    for c in range(IN_WIDTH // IN_COL_CHUNK):
        cols = slice(c * IN_COL_CHUNK, (c + 1) * IN_COL_CHUNK)
        o_ref[:, cols] = _dot(n, w_ref[:, cols]).astype(BF16)


def _in_proj(h, g, w):
    t = h.shape[0]
    return pl.pallas_call(
        _in_proj_kernel,
        grid=(t // ROW_TILE,),
        in_specs=[
            pl.BlockSpec((ROW_TILE, D_MODEL), lambda i: (i, 0)),
            _resident((1, D_MODEL)),
            _resident((D_MODEL, IN_WIDTH)),
        ],
        out_specs=pl.BlockSpec((ROW_TILE, IN_WIDTH), lambda i: (i, 0)),
        out_shape=jax.ShapeDtypeStruct((t, IN_WIDTH), BF16),
        compiler_params=_params("parallel"),
        name="in_proj",
    )(h, g, w)


def _moba_select(gate, qb):
    nb = gate.shape[0]
    row = lax.broadcasted_iota(jnp.int32, gate.shape, 0)
    past = row < qb
    g = jnp.where(past, gate, -jnp.inf)
    rank = jnp.zeros(gate.shape, jnp.int32)
    for m in range(nb):
        gm = g[m:m + 1, :]
        beats = (gm > g) | ((gm == g) & (row > m))
        rank = rank + beats.astype(jnp.int32)
    sel = (past & (rank < MOBA_TOPK)) | (row == qb)
    return jnp.where(sel, 0.0, MASKED)


def _moba_kernel(q_ref, k_ref, v_ref, o_ref, kaug0_ref, kaug1_ref, kmean_ref, *, nb):
    qb = pl.program_id(2)
    blk = MOBA_BLOCK
    lane = lax.broadcasted_iota(jnp.int32, (blk, LANES), 1)
    head0_lane = lane < ATTN_HEAD_DIM

    @pl.when(qb == 0)
    def _():
        lane1 = lax.broadcasted_iota(jnp.int32, (1, LANES), 1)
        for n in range(nb):
            kb = k_ref[0, n * blk:(n + 1) * blk, :]
            mean = jnp.sum(kb.astype(F32), axis=0, keepdims=True) * (1.0 / blk)
            kmean_ref[n:n + 1, :] = jnp.where(lane1 < ATTN_HEAD_DIM, mean, 0.0)
            kmean_ref[nb + n:nb + n + 1, :] = jnp.where(lane1 < ATTN_HEAD_DIM, 0.0, mean)
            kaug0_ref[n * blk:(n + 1) * blk, :] = jnp.where(
                head0_lane, kb, (lane == ATTN_HEAD_DIM + n).astype(BF16))
            kaug1_ref[n * blk:(n + 1) * blk, :] = jnp.where(
                head0_lane, (lane == n).astype(BF16), kb)

    q = q_ref[0]
    km = kmean_ref[...]
    km_hi = km.astype(BF16)
    km_lo = (km - km_hi.astype(F32)).astype(BF16)
    gate = _dot_nt(km_hi, q) + _dot_nt(km_lo, q)
    mask0 = _moba_select(gate[:nb], qb)
    mask1 = _moba_select(gate[nb:], qb)
    pad = lambda r: jnp.zeros((r, blk), F32)
    mask0 = jnp.concatenate([pad(ATTN_HEAD_DIM), mask0, pad(LANES - ATTN_HEAD_DIM - nb)], axis=0)
    mask1 = jnp.concatenate([mask1, pad(LANES - nb)], axis=0)
    qs = (q.astype(F32) * (ATTN_HEAD_DIM ** -0.5)).astype(BF16)
    q0 = jnp.where(head0_lane, qs, mask0.T.astype(BF16))
    q1 = jnp.where(head0_lane, mask1.T.astype(BF16), qs)

    own = pl.multiple_of(qb * blk, blk)
    qpos = lax.broadcasted_iota(jnp.int32, (blk, blk), 0)
    kpos = lax.broadcasted_iota(jnp.int32, (blk, blk), 1)
    causal = kpos <= qpos
    v_own = v_ref[0, pl.ds(own, blk), :]

    def first(qa, kaug_ref):
        s = jnp.where(causal, _dot_nt(qa, kaug_ref[pl.ds(own, blk), :]), MASKED)
        m = jnp.max(s, axis=-1, keepdims=True)
        p = jnp.exp(s - m)
        return m, jnp.sum(p, axis=-1, keepdims=True), _dot(p.astype(BF16), v_own)

    def update(qa, k_blk, v_blk, m, l, acc):
        s = _dot_nt(qa, k_blk)
        m_new = jnp.maximum(m, jnp.max(s, axis=-1, keepdims=True))
        alpha = jnp.exp(m - m_new)
        p = jnp.exp(s - m_new)
        l = alpha * l + jnp.sum(p, axis=-1, keepdims=True)
        acc = alpha * acc + _dot(p.astype(BF16), v_blk)
        return m_new, l, acc

    def body(n, carry):
        m0, l0, a0, m1, l1, a1 = carry
        start = pl.multiple_of(n * blk, blk)
        v_blk = v_ref[0, pl.ds(start, blk), :]
        m0, l0, a0 = update(q0, kaug0_ref[pl.ds(start, blk), :], v_blk, m0, l0, a0)
        m1, l1, a1 = update(q1, kaug1_ref[pl.ds(start, blk), :], v_blk, m1, l1, a1)
        return m0, l0, a0, m1, l1, a1

    init = first(q0, kaug0_ref) + first(q1, kaug1_ref)
    _, l0, a0, _, l1, a1 = lax.fori_loop(0, qb, body, init)
    o_ref[0] = jnp.where(head0_lane, a0 / l0, a1 / l1).astype(BF16)


def _moba(u):
    b, s, _ = u.shape
    nb = s // MOBA_BLOCK
    pairs = ATTN_WIDTH // LANES
    return pl.pallas_call(
        functools.partial(_moba_kernel, nb=nb),
        grid=(b, pairs, nb),
        in_specs=[
            pl.BlockSpec((1, MOBA_BLOCK, LANES), lambda bi, hp, qb: (bi, qb, hp)),
            pl.BlockSpec((1, s, LANES), lambda bi, hp, qb: (bi, 0, pairs + hp)),
            pl.BlockSpec((1, s, LANES), lambda bi, hp, qb: (bi, 0, 2 * pairs + hp)),
        ],
        out_specs=pl.BlockSpec((1, MOBA_BLOCK, LANES), lambda bi, hp, qb: (bi, qb, hp)),
        out_shape=jax.ShapeDtypeStruct((b, s, ATTN_WIDTH), BF16),
        scratch_shapes=[
            pltpu.VMEM((s, LANES), BF16),
            pltpu.VMEM((s, LANES), BF16),
            pltpu.VMEM((2 * nb, LANES), F32),
        ],
        compiler_params=_params("parallel", "parallel", "arbitrary"),
        name="moba",
    )(u, u, u)


def _ret_kernel(q_ref, k_ref, v_ref, g_ref, cos_ref, sin_ref, dmask_ref, kdec_ref, qdec_ref,
                gch_ref, gn_ref, o_ref, state_ref):
    @pl.when(pl.program_id(2) == 0)
    def _():
        state_ref[...] = jnp.zeros_like(state_ref)

    cos = cos_ref[...]
    sin = sin_ref[...]
    lane = lax.broadcasted_iota(jnp.int32, cos.shape, 1)
    even = (lane & 1) == 0

    def rot(x):
        partner = jnp.where(even, pltpu.roll(x, LANES - 1, 1), pltpu.roll(x, 1, 1))
        return x * cos + partner * sin

    q = rot(q_ref[0].astype(F32))
    k = rot(k_ref[0].astype(F32)) * (RET_HEAD_DIM ** -0.5)
    v = v_ref[0]
    state = state_ref[...]
    scores = _dot_nt(q.astype(BF16), k.astype(BF16)) * dmask_ref[0]
    y = _dot(scores.astype(BF16), v) + _dot((q * qdec_ref[0]).astype(BF16), state.astype(BF16))
    k_dec = (k * kdec_ref[0]).T.astype(BF16)
    state_ref[...] = state * gch_ref[0, 0:1, :] + _dot(k_dec, v)
    r = _rms(y, gn_ref[0])
    gate = g_ref[0].astype(F32)
    o_ref[0] = (gate * _sigmoid(gate) * r).astype(BF16)


def _retention(u, ret_norm_g):
    b, s, _ = u.shape
    c = RET_CHUNK
    nc = s // c
    h = RET_HEADS
    base = 3 * ATTN_WIDTH // LANES

    inv = 1.0 / (ROPE_BASE ** jnp.linspace(0.0, 1.0, RET_HEAD_DIM // 2, dtype=F32))
    ang = jnp.arange(s)[:, None].astype(F32) * inv[None, :]
    cos_t = jnp.repeat(jnp.cos(ang), 2, axis=-1)
    sin_t = jnp.repeat(jnp.sin(ang), 2, axis=-1) * jnp.tile(jnp.array([-1.0, 1.0], F32), RET_HEAD_DIM // 2)
    log_g = jnp.log1p(-jnp.exp2(-5.0 - jnp.arange(h, dtype=F32)))
    i = jnp.arange(c, dtype=F32)
    rel = i[:, None] - i[None, :]
    dmask = jnp.where(rel[None] >= 0, jnp.exp(jnp.maximum(rel, 0.0)[None] * log_g[:, None, None]), 0.0)
    k_dec = jnp.exp((c - 1 - i)[None, :] * log_g[:, None])
    q_dec = jnp.exp((i + 1.0)[None, :] * log_g[:, None])
    g_chunk = jnp.exp(c * log_g)
    k_dec = jnp.broadcast_to(k_dec[:, :, None], (h, c, LANES))
    q_dec = jnp.broadcast_to(q_dec[:, :, None], (h, c, LANES))
    g_chunk = jnp.broadcast_to(g_chunk[:, None, None], (h, 8, LANES))
    gn = ret_norm_g.reshape(h, 1, RET_HEAD_DIM)

    def col(off):
        return pl.BlockSpec((1, c, LANES), lambda bi, hi, n: (bi, n, base + off * h + hi))

    per_head = lambda r: pl.BlockSpec((1, r, LANES), lambda bi, hi, n: (hi, 0, 0))
    return pl.pallas_call(
        _ret_kernel,
        grid=(b, h, nc),
        in_specs=[
            col(0), col(1), col(2), col(3),
            pl.BlockSpec((c, LANES), lambda bi, hi, n: (n, 0)),
            pl.BlockSpec((c, LANES), lambda bi, hi, n: (n, 0)),
            pl.BlockSpec((1, c, c), lambda bi, hi, n: (hi, 0, 0)),
            per_head(c), per_head(c), per_head(8), per_head(1),
        ],
        out_specs=pl.BlockSpec((1, c, LANES), lambda bi, hi, n: (bi, n, hi)),
        out_shape=jax.ShapeDtypeStruct((b, s, RET_WIDTH), BF16),
        scratch_shapes=[pltpu.VMEM((RET_HEAD_DIM, RET_HEAD_DIM), F32)],
        compiler_params=_params("parallel", "parallel", "arbitrary"),
        name="retention",
    )(u, u, u, u, cos_t, sin_t, dmask, k_dec, q_dec, g_chunk, gn)


def _out_proj_kernel(h_ref, a_ref, r_ref, wa_ref, wr_ref, o_ref):
    o_ref[...] = h_ref[...] + _dot(a_ref[...], wa_ref[...]) + _dot(r_ref[...], wr_ref[...])


def _out_proj(h, a, r, wa, wr):
    t = h.shape[0]
    row = lambda w: pl.BlockSpec((ROW_TILE, w), lambda i: (i, 0))
    return pl.pallas_call(
        _out_proj_kernel,
        grid=(t // ROW_TILE,),
        in_specs=[row(D_MODEL), row(ATTN_WIDTH), row(RET_WIDTH),
                  _resident((ATTN_WIDTH, D_MODEL)), _resident((RET_WIDTH, D_MODEL))],
        out_specs=row(D_MODEL),
        out_shape=jax.ShapeDtypeStruct((t, D_MODEL), F32),
        compiler_params=_params("parallel"),
        name="out_proj",
    )(h, a, r, wa, wr)


def _ffn_kernel(h_ref, g_ref, wg_ref, wu_ref, wo_ref, o_ref, n_ref):
    h = h_ref[...]
    n_ref[...] = _rms(h, g_ref[...]).astype(BF16)
    o_ref[...] = h

    def body(c, carry):
        n = n_ref[...]
        zg = _dot(n, wg_ref[c])
        zu = _dot(n, wu_ref[c])
        act = (zg * _sigmoid(zg) * zu).astype(BF16)
        o_ref[...] += _dot(act, wo_ref[c])
        return carry

    lax.fori_loop(0, N_FF_CHUNKS, body, 0)


def _ffn(h, g, wg, wu, wo):
    t = h.shape[0]
    row = pl.BlockSpec((ROW_TILE, D_MODEL), lambda i: (i, 0))
    return pl.pallas_call(
        _ffn_kernel,
        grid=(t // ROW_TILE,),
        in_specs=[row, _resident((1, D_MODEL)),
                  _resident((N_FF_CHUNKS, D_MODEL, FF_CHUNK)),
                  _resident((N_FF_CHUNKS, D_MODEL, FF_CHUNK)),
                  _resident((N_FF_CHUNKS, FF_CHUNK, D_MODEL))],
        out_specs=row,
        out_shape=jax.ShapeDtypeStruct((t, D_MODEL), F32),
        scratch_shapes=[pltpu.VMEM((ROW_TILE, D_MODEL), BF16)],
        compiler_params=_params("parallel"),
        name="ffn",
    )(h, g, wg, wu, wo)


def _ple_kernel(h_ref, p_ref, g_ref, wpg_ref, wpp_ref, fg_ref, o_ref, *, final_norm):
    h = h_ref[...]
    gate = _sigmoid(_dot(_rms(h, g_ref[...]).astype(BF16), wpg_ref[...]))
    out = h + gate * _dot(p_ref[...].astype(BF16), wpp_ref[...])
    if final_norm:
        out = _rms(out, fg_ref[...])
    o_ref[...] = out


def _ple(h, p, g, wpg, wpp, fg, final_norm):
    t = h.shape[0]
    row = lambda w: pl.BlockSpec((ROW_TILE, w), lambda i: (i, 0))
    return pl.pallas_call(
        functools.partial(_ple_kernel, final_norm=final_norm),
        grid=(t // ROW_TILE,),
        in_specs=[row(D_MODEL), row(PLE_DIM), _resident((1, D_MODEL)),
                  _resident((D_MODEL, D_MODEL)), _resident((PLE_DIM, D_MODEL)),
                  _resident((1, D_MODEL))],
        out_specs=row(D_MODEL),
        out_shape=jax.ShapeDtypeStruct((t, D_MODEL), F32),
        compiler_params=_params("parallel"),
        name="ple",
    )(h, p, g, wpg, wpp, fg)


def kernel(x, p, attn_norm_g, w_in, ret_norm_g, w_out, ffn_norm_g, w_ffn_in, w_ffn_out,
           ple_norm_g, w_ple_gate, w_ple_proj, final_norm_g):
    b, s, d = x.shape
    depth = p.shape[0]
    assert d == D_MODEL and s % MOBA_BLOCK == 0 and s >= (MOBA_TOPK + 1) * MOBA_BLOCK
    assert (b * s) % ROW_TILE == 0 and s // MOBA_BLOCK <= LANES - ATTN_HEAD_DIM
    t = b * s
    h = x.reshape(t, d)
    gain = lambda g: g.reshape(1, -1).astype(F32)
    for i in range(depth):
        u = _in_proj(h, gain(attn_norm_g[i]), w_in[i].astype(BF16)).reshape(b, s, IN_WIDTH)
        a = _moba(u).reshape(t, ATTN_WIDTH)
        r = _retention(u, ret_norm_g[i].astype(F32)).reshape(t, RET_WIDTH)
        wo = w_out[i].astype(BF16)
        h = _out_proj(h, a, r, wo[:ATTN_WIDTH], wo[ATTN_WIDTH:])
        wf = w_ffn_in[i].astype(BF16)
        chunked = lambda w: w.reshape(D_MODEL, N_FF_CHUNKS, FF_CHUNK).transpose(1, 0, 2)
        h = _ffn(h, gain(ffn_norm_g[i]), chunked(wf[:, :D_FF]), chunked(wf[:, D_FF:]),
                 w_ffn_out[i].astype(BF16).reshape(N_FF_CHUNKS, FF_CHUNK, D_MODEL))
        h = _ple(h, p[i].reshape(t, PLE_DIM), gain(ple_norm_g[i]), w_ple_gate[i].astype(BF16),
                 w_ple_proj[i].astype(BF16), gain(final_norm_g), final_norm=(i == depth - 1))
    return h.reshape(b, s, d)
```

```python
import functools

import jax
import jax.numpy as jnp
from jax import lax
from jax.experimental import pallas as pl
from jax.experimental.pallas import tpu as pltpu

D_MODEL = 1024
PLE_DIM = 256
ATTN_WIDTH = 512
ATTN_HEAD_DIM = 64
RET_WIDTH = 512
RET_HEADS = 4
RET_HEAD_DIM = 128
IN_WIDTH = 3 * ATTN_WIDTH + 4 * RET_WIDTH
MOBA_BLOCK = 256
MOBA_TOPK = 3
RET_CHUNK = 256
ROPE_BASE = 10000.0
D_FF = 2816
EPS = 1e-6

LANES = 128
FF_CHUNK = 256
N_FF_CHUNKS = D_FF // FF_CHUNK
ROW_TILE = 512
IN_COL_CHUNK = 512
MASKED = -1e30
VMEM_LIMIT = 52 * 1024 * 1024

F32 = jnp.float32
BF16 = jnp.bfloat16


def _dot(a, b):
    return jnp.dot(a, b, preferred_element_type=F32)


def _dot_nt(a, b):
    return lax.dot_general(a, b, (((1,), (1,)), ((), ())), preferred_element_type=F32)


def _rms(x, g):
    return x * lax.rsqrt(jnp.mean(x * x, axis=-1, keepdims=True) + EPS) * g


def _sigmoid(x):
    return 1.0 / (1.0 + jnp.exp(-x))


def _resident(shape):
    zeros = (0,) * len(shape)
    return pl.BlockSpec(shape, lambda *_: zeros, pipeline_mode=pl.Buffered(1))


def _params(*sem):
    return pltpu.CompilerParams(dimension_semantics=sem, vmem_limit_bytes=VMEM_LIMIT)


def _in_proj_kernel(x_ref, g_ref, w_ref, o_ref):
    n = _rms(x_ref[...], g_ref[...]).astype(BF16)
    for c in range(IN_WIDTH // IN_COL_CHUNK):
        cols = slice(c * IN_COL_CHUNK, (c + 1) * IN_COL_CHUNK)
        o_ref[:, cols] = _dot(n, w_ref[:, cols]).astype(BF16)


def _in_proj(h, g, w):
    t = h.shape[0]
    return pl.pallas_call(
        _in_proj_kernel,
        grid=(t // ROW_TILE,),
        in_specs=[
            pl.BlockSpec((ROW_TILE, D_MODEL), lambda i: (i, 0)),
            _resident((1, D_MODEL)),
            _resident((D_MODEL, IN_WIDTH)),
        ],
        out_specs=pl.BlockSpec((ROW_TILE, IN_WIDTH), lambda i: (i, 0)),
        out_shape=jax.ShapeDtypeStruct((t, IN_WIDTH), BF16),
        compiler_params=_params("parallel"),
        name="in_proj",
    )(h, g, w)


ONES_ROWS = 16
VT_ROWS = ATTN_HEAD_DIM + ONES_ROWS


def _moba_select(gate, qb):
    nb = gate.shape[0]
    row = lax.broadcasted_iota(jnp.int32, gate.shape, 0)
    past = row < qb
    g = jnp.where(past, gate, -jnp.inf)
    rank = jnp.zeros(gate.shape, jnp.int32)
    for m in range(nb):
        gm = g[m:m + 1, :]
        beats = (gm > g) | ((gm == g) & (row > m))
        rank = rank + beats.astype(jnp.int32)
    return jnp.where(past & (rank < MOBA_TOPK), 0.0, MASKED)


def _moba_kernel(q_ref, k_ref, v_ref, o_ref, kaug0_ref, kaug1_ref, vt0_ref, vt1_ref, kmean_ref,
                 *, nb):
    qb = pl.program_id(2)
    blk = MOBA_BLOCK
    hd = ATTN_HEAD_DIM

    @pl.when(qb == 0)
    def _():
        lane = lax.broadcasted_iota(jnp.int32, (blk, LANES), 1)
        head0_lane = lane < hd
        lane1 = lax.broadcasted_iota(jnp.int32, (1, LANES), 1)
        ones = jnp.ones((ONES_ROWS, blk), BF16)
        for n in range(nb):
            rows = slice(n * blk, (n + 1) * blk)
            kb = k_ref[0, rows, :]
            mean = jnp.sum(kb.astype(F32), axis=0, keepdims=True) * (1.0 / blk)
            kmean_ref[n:n + 1, :] = jnp.where(lane1 < hd, mean, 0.0)
            kmean_ref[nb + n:nb + n + 1, :] = jnp.where(lane1 < hd, 0.0, mean)
            kaug0_ref[n] = jnp.where(head0_lane, kb, (lane == hd + n).astype(BF16))
            kaug1_ref[n] = jnp.where(head0_lane, (lane == n).astype(BF16), kb)
            vt = v_ref[0, rows, :].astype(F32).T.astype(BF16)
            vt0_ref[n] = jnp.concatenate([vt[:hd], ones], axis=0)
            vt1_ref[n] = jnp.concatenate([vt[hd:], ones], axis=0)

    qt = (q_ref[0].astype(F32) * (hd ** -0.5)).T.astype(BF16)
    km = kmean_ref[...]
    km_hi = km.astype(BF16)
    km_lo = (km - km_hi.astype(F32)).astype(BF16)
    gate = _dot(km_hi, qt) + _dot(km_lo, qt)
    mask0 = _moba_select(gate[:nb], qb).astype(BF16)
    mask1 = _moba_select(gate[nb:], qb).astype(BF16)
    zeros = lambda r: jnp.zeros((r, blk), BF16)
    qt_past0 = jnp.concatenate([qt[:hd], mask0, zeros(LANES - hd - nb)], axis=0)
    qt_past1 = jnp.concatenate([mask1, zeros(hd - nb), qt[hd:]], axis=0)
    qt_own0 = jnp.concatenate([qt[:hd], zeros(LANES - hd)], axis=0)
    qt_own1 = jnp.concatenate([zeros(hd), qt[hd:]], axis=0)

    def block_partials(jobs, keep=None):
        logits = [_dot(kaug_ref[n], qt_aug) for kaug_ref, _, n, qt_aug in jobs]
        if keep is not None:
            logits = [jnp.where(keep, s, MASKED) for s in logits]
        maxes = [jnp.max(s, axis=0, keepdims=True) for s in logits]
        probs = [jnp.exp(s - m).astype(BF16) for s, m in zip(logits, maxes)]
        return [(m, _dot(job[1][job[2]], p)) for job, m, p in zip(jobs, maxes, probs)]

    def merge(m, acc, parts):
        m_new = m
        for mp, _ in parts:
            m_new = jnp.maximum(m_new, mp)
        acc = acc * jnp.exp(m - m_new)
        for mp, rp in parts:
            acc = acc + rp * jnp.exp(mp - m_new)
        return m_new, acc

    kpos = lax.broadcasted_iota(jnp.int32, (blk, blk), 0)
    qpos = lax.broadcasted_iota(jnp.int32, (blk, blk), 1)
    causal = kpos <= qpos
    own0, own1 = block_partials([(kaug0_ref, vt0_ref, qb, qt_own0),
                                 (kaug1_ref, vt1_ref, qb, qt_own1)], causal)
    init = own0 + own1

    def body(i, carry):
        m0, a0, m1, a1 = carry
        parts = block_partials([(kaug0_ref, vt0_ref, 2 * i, qt_past0),
                                (kaug1_ref, vt1_ref, 2 * i, qt_past1),
                                (kaug0_ref, vt0_ref, 2 * i + 1, qt_past0),
                                (kaug1_ref, vt1_ref, 2 * i + 1, qt_past1)])
        return merge(m0, a0, parts[0::2]) + merge(m1, a1, parts[1::2])

    _, a0, _, a1 = lax.fori_loop(0, (qb + 1) // 2, body, init)
    out_t = jnp.concatenate([a0[:hd] / a0[hd:hd + 1], a1[:hd] / a1[hd:hd + 1]], axis=0)
    o_ref[0] = out_t.T.astype(BF16)


def _moba(u):
    b, s, _ = u.shape
    nb = s // MOBA_BLOCK
    assert nb % ONES_ROWS == 0 and nb <= ATTN_HEAD_DIM
    pairs = ATTN_WIDTH // LANES
    return pl.pallas_call(
        functools.partial(_moba_kernel, nb=nb),
        grid=(b, pairs, nb),
        in_specs=[
            pl.BlockSpec((1, MOBA_BLOCK, LANES), lambda bi, hp, qb: (bi, qb, hp)),
            pl.BlockSpec((1, s, LANES), lambda bi, hp, qb: (bi, 0, pairs + hp)),
            pl.BlockSpec((1, s, LANES), lambda bi, hp, qb: (bi, 0, 2 * pairs + hp)),
        ],
        out_specs=pl.BlockSpec((1, MOBA_BLOCK, LANES), lambda bi, hp, qb: (bi, qb, hp)),
        out_shape=jax.ShapeDtypeStruct((b, s, ATTN_WIDTH), BF16),
        scratch_shapes=[
            pltpu.VMEM((nb, MOBA_BLOCK, LANES), BF16),
            pltpu.VMEM((nb, MOBA_BLOCK, LANES), BF16),
            pltpu.VMEM((nb, VT_ROWS, MOBA_BLOCK), BF16),
            pltpu.VMEM((nb, VT_ROWS, MOBA_BLOCK), BF16),
            pltpu.VMEM((2 * nb, LANES), F32),
        ],
        compiler_params=_params("parallel", "parallel", "arbitrary"),
        name="moba",
    )(u, u, u)


def _ret_kernel(q_ref, k_ref, v_ref, g_ref, cos_ref, sin_ref, dmask_ref, kdec_ref, qdec_ref,
                gch_ref, gn_ref, o_ref, state_ref):
    @pl.when(pl.program_id(2) == 0)
    def _():
        state_ref[...] = jnp.zeros_like(state_ref)

    cos = cos_ref[...]
    sin = sin_ref[...]
    lane = lax.broadcasted_iota(jnp.int32, cos.shape, 1)
    even = (lane & 1) == 0

    def rot(x):
        partner = jnp.where(even, pltpu.roll(x, LANES - 1, 1), pltpu.roll(x, 1, 1))
        return x * cos + partner * sin

    q = rot(q_ref[0].astype(F32))
    k = rot(k_ref[0].astype(F32)) * (RET_HEAD_DIM ** -0.5)
    v = v_ref[0]
    state = state_ref[...]
    scores = _dot_nt(q.astype(BF16), k.astype(BF16)) * dmask_ref[0]
    y = _dot(scores.astype(BF16), v) + _dot((q * qdec_ref[0]).astype(BF16), state.astype(BF16))
    k_dec = (k * kdec_ref[0]).T.astype(BF16)
    state_ref[...] = state * gch_ref[0, 0:1, :] + _dot(k_dec, v)
    r = _rms(y, gn_ref[0])
    gate = g_ref[0].astype(F32)
    o_ref[0] = (gate * _sigmoid(gate) * r).astype(BF16)


def _retention(u, ret_norm_g):
    b, s, _ = u.shape
    c = RET_CHUNK
    nc = s // c
    h = RET_HEADS
    base = 3 * ATTN_WIDTH // LANES

    inv = 1.0 / (ROPE_BASE ** jnp.linspace(0.0, 1.0, RET_HEAD_DIM // 2, dtype=F32))
    ang = jnp.arange(s)[:, None].astype(F32) * inv[None, :]
    cos_t = jnp.repeat(jnp.cos(ang), 2, axis=-1)
    sin_t = jnp.repeat(jnp.sin(ang), 2, axis=-1) * jnp.tile(jnp.array([-1.0, 1.0], F32), RET_HEAD_DIM // 2)
    log_g = jnp.log1p(-jnp.exp2(-5.0 - jnp.arange(h, dtype=F32)))
    i = jnp.arange(c, dtype=F32)
    rel = i[:, None] - i[None, :]
    dmask = jnp.where(rel[None] >= 0, jnp.exp(jnp.maximum(rel, 0.0)[None] * log_g[:, None, None]), 0.0)
    k_dec = jnp.exp((c - 1 - i)[None, :] * log_g[:, None])
    q_dec = jnp.exp((i + 1.0)[None, :] * log_g[:, None])
    g_chunk = jnp.exp(c * log_g)
    k_dec = jnp.broadcast_to(k_dec[:, :, None], (h, c, LANES))
    q_dec = jnp.broadcast_to(q_dec[:, :, None], (h, c, LANES))
    g_chunk = jnp.broadcast_to(g_chunk[:, None, None], (h, 8, LANES))
    gn = ret_norm_g.reshape(h, 1, RET_HEAD_DIM)

    def col(off):
        return pl.BlockSpec((1, c, LANES), lambda bi, hi, n: (bi, n, base + off * h + hi))

    per_head = lambda r: pl.BlockSpec((1, r, LANES), lambda bi, hi, n: (hi, 0, 0))
    return pl.pallas_call(
        _ret_kernel,
        grid=(b, h, nc),
        in_specs=[
            col(0), col(1), col(2), col(3),
            pl.BlockSpec((c, LANES), lambda bi, hi, n: (n, 0)),
            pl.BlockSpec((c, LANES), lambda bi, hi, n: (n, 0)),
            pl.BlockSpec((1, c, c), lambda bi, hi, n: (hi, 0, 0)),
            per_head(c), per_head(c), per_head(8), per_head(1),
        ],
        out_specs=pl.BlockSpec((1, c, LANES), lambda bi, hi, n: (bi, n, hi)),
        out_shape=jax.ShapeDtypeStruct((b, s, RET_WIDTH), BF16),
        scratch_shapes=[pltpu.VMEM((RET_HEAD_DIM, RET_HEAD_DIM), F32)],
        compiler_params=_params("parallel", "parallel", "arbitrary"),
        name="retention",
    )(u, u, u, u, cos_t, sin_t, dmask, k_dec, q_dec, g_chunk, gn)


def _out_proj_kernel(h_ref, a_ref, r_ref, wa_ref, wr_ref, o_ref):
    o_ref[...] = h_ref[...] + _dot(a_ref[...], wa_ref[...]) + _dot(r_ref[...], wr_ref[...])


def _out_proj(h, a, r, wa, wr):
    t = h.shape[0]
    row = lambda w: pl.BlockSpec((ROW_TILE, w), lambda i: (i, 0))
    return pl.pallas_call(
        _out_proj_kernel,
        grid=(t // ROW_TILE,),
        in_specs=[row(D_MODEL), row(ATTN_WIDTH), row(RET_WIDTH),
                  _resident((ATTN_WIDTH, D_MODEL)), _resident((RET_WIDTH, D_MODEL))],
        out_specs=row(D_MODEL),
        out_shape=jax.ShapeDtypeStruct((t, D_MODEL), F32),
        compiler_params=_params("parallel"),
        name="out_proj",
    )(h, a, r, wa, wr)


def _ffn_kernel(h_ref, g_ref, wg_ref, wu_ref, wo_ref, o_ref, n_ref):
    h = h_ref[...]
    n_ref[...] = _rms(h, g_ref[...]).astype(BF16)
    o_ref[...] = h

    def body(c, carry):
        n = n_ref[...]
        zg = _dot(n, wg_ref[c])
        zu = _dot(n, wu_ref[c])
        act = (zg * _sigmoid(zg) * zu).astype(BF16)
        o_ref[...] += _dot(act, wo_ref[c])
        return carry

    lax.fori_loop(0, N_FF_CHUNKS, body, 0)


def _ffn(h, g, wg, wu, wo):
    t = h.shape[0]
    row = pl.BlockSpec((ROW_TILE, D_MODEL), lambda i: (i, 0))
    return pl.pallas_call(
        _ffn_kernel,
        grid=(t // ROW_TILE,),
        in_specs=[row, _resident((1, D_MODEL)),
                  _resident((N_FF_CHUNKS, D_MODEL, FF_CHUNK)),
                  _resident((N_FF_CHUNKS, D_MODEL, FF_CHUNK)),
                  _resident((N_FF_CHUNKS, FF_CHUNK, D_MODEL))],
        out_specs=row,
        out_shape=jax.ShapeDtypeStruct((t, D_MODEL), F32),
        scratch_shapes=[pltpu.VMEM((ROW_TILE, D_MODEL), BF16)],
        compiler_params=_params("parallel"),
        name="ffn",
    )(h, g, wg, wu, wo)


def _ple_kernel(h_ref, p_ref, g_ref, wpg_ref, wpp_ref, fg_ref, o_ref, *, final_norm):
    h = h_ref[...]
    gate = _sigmoid(_dot(_rms(h, g_ref[...]).astype(BF16), wpg_ref[...]))
    out = h + gate * _dot(p_ref[...].astype(BF16), wpp_ref[...])
    if final_norm:
        out = _rms(out, fg_ref[...])
    o_ref[...] = out


def _ple(h, p, g, wpg, wpp, fg, final_norm):
    t = h.shape[0]
    row = lambda w: pl.BlockSpec((ROW_TILE, w), lambda i: (i, 0))
    return pl.pallas_call(
        functools.partial(_ple_kernel, final_norm=final_norm),
        grid=(t // ROW_TILE,),
        in_specs=[row(D_MODEL), row(PLE_DIM), _resident((1, D_MODEL)),
                  _resident((D_MODEL, D_MODEL)), _resident((PLE_DIM, D_MODEL)),
                  _resident((1, D_MODEL))],
        out_specs=row(D_MODEL),
        out_shape=jax.ShapeDtypeStruct((t, D_MODEL), F32),
        compiler_params=_params("parallel"),
        name="ple",
    )(h, p, g, wpg, wpp, fg)


def kernel(x, p, attn_norm_g, w_in, ret_norm_g, w_out, ffn_norm_g, w_ffn_in, w_ffn_out,
           ple_norm_g, w_ple_gate, w_ple_proj, final_norm_g):
    b, s, d = x.shape
    depth = p.shape[0]
    assert d == D_MODEL and s % MOBA_BLOCK == 0 and s >= (MOBA_TOPK + 1) * MOBA_BLOCK
    assert (b * s) % ROW_TILE == 0 and s // MOBA_BLOCK <= LANES - ATTN_HEAD_DIM
    t = b * s
    h = x.reshape(t, d)
    gain = lambda g: g.reshape(1, -1).astype(F32)
    for i in range(depth):
        u = _in_proj(h, gain(attn_norm_g[i]), w_in[i].astype(BF16)).reshape(b, s, IN_WIDTH)
        a = _moba(u).reshape(t, ATTN_WIDTH)
        r = _retention(u, ret_norm_g[i].astype(F32)).reshape(t, RET_WIDTH)
        wo = w_out[i].astype(BF16)
        h = _out_proj(h, a, r, wo[:ATTN_WIDTH], wo[ATTN_WIDTH:])
        wf = w_ffn_in[i].astype(BF16)
        chunked = lambda w: w.reshape(D_MODEL, N_FF_CHUNKS, FF_CHUNK).transpose(1, 0, 2)
        h = _ffn(h, gain(ffn_norm_g[i]), chunked(wf[:, :D_FF]), chunked(wf[:, D_FF:]),
                 w_ffn_out[i].astype(BF16).reshape(N_FF_CHUNKS, FF_CHUNK, D_MODEL))
        h = _ple(h, p[i].reshape(t, PLE_DIM), gain(ple_norm_g[i]), w_ple_gate[i].astype(BF16),
                 w_ple_proj[i].astype(BF16), gain(final_norm_g), final_norm=(i == depth - 1))
    return h.reshape(b, s, d)
```

```python
import functools

import jax
import jax.numpy as jnp
from jax import lax
from jax.experimental import pallas as pl
from jax.experimental.pallas import tpu as pltpu

D_MODEL = 1024
PLE_DIM = 256
ATTN_WIDTH = 512
ATTN_HEAD_DIM = 64
RET_WIDTH = 512
RET_HEADS = 4
RET_HEAD_DIM = 128
IN_WIDTH = 3 * ATTN_WIDTH + 4 * RET_WIDTH
MOBA_BLOCK = 256
MOBA_TOPK = 3
RET_CHUNK = 256
ROPE_BASE = 10000.0
D_FF = 2816
EPS = 1e-6

LANES = 128
FF_CHUNK = 256
N_FF_CHUNKS = D_FF // FF_CHUNK
ROW_TILE = 512
IN_COL_CHUNK = 512
MASKED = -1e30
LOG2_E = 1.4426950408889634
VMEM_LIMIT = 52 * 1024 * 1024

F32 = jnp.float32
BF16 = jnp.bfloat16


def _dot(a, b):
    return jnp.dot(a, b, preferred_element_type=F32)


def _dot_nt(a, b):
    return lax.dot_general(a, b, (((1,), (1,)), ((), ())), preferred_element_type=F32)


def _rms(x, g):
    return x * lax.rsqrt(jnp.mean(x * x, axis=-1, keepdims=True) + EPS) * g


def _sigmoid(x):
    return 1.0 / (1.0 + jnp.exp(-x))


def _resident(shape):
    zeros = (0,) * len(shape)
    return pl.BlockSpec(shape, lambda *_: zeros, pipeline_mode=pl.Buffered(1))


def _params(*sem):
    return pltpu.CompilerParams(dimension_semantics=sem, vmem_limit_bytes=VMEM_LIMIT)


def _in_proj_kernel(x_ref, g_ref, w_ref, o_ref):
    n = _rms(x_ref[...], g_ref[...]).astype(BF16)
    for c in range(IN_WIDTH // IN_COL_CHUNK):
        cols = slice(c * IN_COL_CHUNK, (c + 1) * IN_COL_CHUNK)
        o_ref[:, cols] = _dot(n, w_ref[:, cols]).astype(BF16)


def _in_proj(h, g, w):
    t = h.shape[0]
    return pl.pallas_call(
        _in_proj_kernel,
        grid=(t // ROW_TILE,),
        in_specs=[
            pl.BlockSpec((ROW_TILE, D_MODEL), lambda i: (i, 0)),
            _resident((1, D_MODEL)),
            _resident((D_MODEL, IN_WIDTH)),
        ],
        out_specs=pl.BlockSpec((ROW_TILE, IN_WIDTH), lambda i: (i, 0)),
        out_shape=jax.ShapeDtypeStruct((t, IN_WIDTH), BF16),
        compiler_params=_params("parallel"),
        name="in_proj",
    )(h, g, w)


ONES_ROWS = 16
VT_ROWS = ATTN_HEAD_DIM + ONES_ROWS


def _moba_select(gate, qb):
    nb = gate.shape[0]
    row = lax.broadcasted_iota(jnp.int32, gate.shape, 0)
    past = row < qb
    g = jnp.where(past, gate, -jnp.inf)
    rank = jnp.zeros(gate.shape, jnp.int32)
    for m in range(nb):
        gm = g[m:m + 1, :]
        beats = (gm > g) | ((gm == g) & (row > m))
        rank = rank + beats.astype(jnp.int32)
    return jnp.where(past & (rank < MOBA_TOPK), 0.0, MASKED)


def _moba_kernel(q_ref, k_ref, v_ref, o_ref, kaug0_ref, kaug1_ref, vt0_ref, vt1_ref, kmean_ref,
                 sa_ref, sb_ref, *, nb):
    qb = pl.program_id(2)
    blk = MOBA_BLOCK
    hd = ATTN_HEAD_DIM

    @pl.when(qb == 0)
    def _():
        lane = lax.broadcasted_iota(jnp.int32, (blk, LANES), 1)
        head0_lane = lane < hd
        lane1 = lax.broadcasted_iota(jnp.int32, (1, LANES), 1)
        ones = jnp.ones((ONES_ROWS, blk), BF16)
        for n in range(nb):
            rows = slice(n * blk, (n + 1) * blk)
            kb = k_ref[0, rows, :]
            mean = jnp.sum(kb.astype(F32), axis=0, keepdims=True) * (1.0 / blk)
            kmean_ref[n:n + 1, :] = jnp.where(lane1 < hd, mean, 0.0)
            kmean_ref[nb + n:nb + n + 1, :] = jnp.where(lane1 < hd, 0.0, mean)
            kaug0_ref[n] = jnp.where(head0_lane, kb, (lane == hd + n).astype(BF16))
            kaug1_ref[n] = jnp.where(head0_lane, (lane == n).astype(BF16), kb)
            vt = v_ref[0, rows, :].astype(F32).T.astype(BF16)
            vt0_ref[n] = jnp.concatenate([vt[:hd], ones], axis=0)
            vt1_ref[n] = jnp.concatenate([vt[hd:], ones], axis=0)

    qt = (q_ref[0].astype(F32) * (hd ** -0.5 * LOG2_E)).T.astype(BF16)
    km = kmean_ref[...]
    km_hi = km.astype(BF16)
    km_lo = (km - km_hi.astype(F32)).astype(BF16)
    gate = _dot(km_hi, qt) + _dot(km_lo, qt)
    mask0 = _moba_select(gate[:nb], qb).astype(BF16)
    mask1 = _moba_select(gate[nb:], qb).astype(BF16)
    zeros = lambda r: jnp.zeros((r, blk), BF16)
    qt_past0 = jnp.concatenate([qt[:hd], mask0, zeros(LANES - hd - nb)], axis=0)
    qt_past1 = jnp.concatenate([mask1, zeros(hd - nb), qt[hd:]], axis=0)
    qt_own0 = jnp.concatenate([qt[:hd], zeros(LANES - hd)], axis=0)
    qt_own1 = jnp.concatenate([zeros(hd), qt[hd:]], axis=0)

    def partials(logits, vts):
        maxes = [jnp.max(s, axis=0, keepdims=True) for s in logits]
        probs = [jnp.exp2(s - m).astype(BF16) for s, m in zip(logits, maxes)]
        return [(m, _dot(vt, p)) for m, vt, p in zip(maxes, vts, probs)]

    def merge(m, acc, parts):
        m_new = m
        for mp, _ in parts:
            m_new = jnp.maximum(m_new, mp)
        acc = acc * jnp.exp2(m - m_new)
        for mp, rp in parts:
            acc = acc + rp * jnp.exp2(mp - m_new)
        return m_new, acc

    def issue_logits(s_ref, pair):
        for j in range(2):
            n = jnp.minimum(2 * pair + j, nb - 1)
            s_ref[2 * j] = _dot(kaug0_ref[n], qt_past0)
            s_ref[2 * j + 1] = _dot(kaug1_ref[n], qt_past1)

    def consume(s_ref, pair, carry):
        m0, a0, m1, a1 = carry
        blocks = [jnp.minimum(2 * pair + j, nb - 1) for j in range(2)]
        vts = [vt0_ref[blocks[0]], vt1_ref[blocks[0]], vt0_ref[blocks[1]], vt1_ref[blocks[1]]]
        parts = partials([s_ref[c] for c in range(4)], vts)
        return merge(m0, a0, parts[0::2]) + merge(m1, a1, parts[1::2])

    issue_logits(sa_ref, 0)

    kpos = lax.broadcasted_iota(jnp.int32, (blk, blk), 0)
    qpos = lax.broadcasted_iota(jnp.int32, (blk, blk), 1)
    causal = kpos <= qpos
    own_logits = [jnp.where(causal, _dot(kaug0_ref[qb], qt_own0), MASKED),
                  jnp.where(causal, _dot(kaug1_ref[qb], qt_own1), MASKED)]
    own0, own1 = partials(own_logits, [vt0_ref[qb], vt1_ref[qb]])

    def body(t, carry):
        issue_logits(sb_ref, 2 * t + 1)
        carry = consume(sa_ref, 2 * t, carry)
        issue_logits(sa_ref, 2 * t + 2)
        return consume(sb_ref, 2 * t + 1, carry)

    n_pairs = (qb + 1) // 2
    _, a0, _, a1 = lax.fori_loop(0, (n_pairs + 1) // 2, body, own0 + own1)
    out_t = jnp.concatenate([a0[:hd] / a0[hd:hd + 1], a1[:hd] / a1[hd:hd + 1]], axis=0)
    o_ref[0] = out_t.T.astype(BF16)


def _moba(u):
    b, s, _ = u.shape
    nb = s // MOBA_BLOCK
    assert nb % ONES_ROWS == 0 and nb <= ATTN_HEAD_DIM
    pairs = ATTN_WIDTH // LANES
    return pl.pallas_call(
        functools.partial(_moba_kernel, nb=nb),
        grid=(b, pairs, nb),
        in_specs=[
            pl.BlockSpec((1, MOBA_BLOCK, LANES), lambda bi, hp, qb: (bi, qb, hp)),
            pl.BlockSpec((1, s, LANES), lambda bi, hp, qb: (bi, 0, pairs + hp)),
            pl.BlockSpec((1, s, LANES), lambda bi, hp, qb: (bi, 0, 2 * pairs + hp)),
        ],
        out_specs=pl.BlockSpec((1, MOBA_BLOCK, LANES), lambda bi, hp, qb: (bi, qb, hp)),
        out_shape=jax.ShapeDtypeStruct((b, s, ATTN_WIDTH), BF16),
        scratch_shapes=[
            pltpu.VMEM((nb, MOBA_BLOCK, LANES), BF16),
            pltpu.VMEM((nb, MOBA_BLOCK, LANES), BF16),
            pltpu.VMEM((nb, VT_ROWS, MOBA_BLOCK), BF16),
            pltpu.VMEM((nb, VT_ROWS, MOBA_BLOCK), BF16),
            pltpu.VMEM((2 * nb, LANES), F32),
            pltpu.VMEM((4, MOBA_BLOCK, MOBA_BLOCK), F32),
            pltpu.VMEM((4, MOBA_BLOCK, MOBA_BLOCK), F32),
        ],
        compiler_params=_params("parallel", "parallel", "arbitrary"),
        name="moba",
    )(u, u, u)


def _ret_kernel(q_ref, k_ref, v_ref, g_ref, cos_ref, sin_ref, dmask_ref, kdec_ref, qdec_ref,
                gch_ref, gn_ref, o_ref, state_ref):
    @pl.when(pl.program_id(1) == 0)
    def _():
        state_ref[...] = jnp.zeros_like(state_ref)

    cos = cos_ref[...]
    sin = sin_ref[...]
    lane = lax.broadcasted_iota(jnp.int32, cos.shape, 1)
    even = (lane & 1) == 0

    def rot(x):
        partner = jnp.where(even, pltpu.roll(x, LANES - 1, 1), pltpu.roll(x, 1, 1))
        return x * cos + partner * sin

    heads = range(RET_HEADS)
    cols = [slice(h * RET_HEAD_DIM, (h + 1) * RET_HEAD_DIM) for h in heads]
    q = [rot(q_ref[0, :, c].astype(F32)) for c in cols]
    k = [rot(k_ref[0, :, c].astype(F32)) * (RET_HEAD_DIM ** -0.5) for c in cols]
    v = [v_ref[0, :, c] for c in cols]
    state = [state_ref[h] for h in heads]
    scores = [_dot_nt(q[h].astype(BF16), k[h].astype(BF16)) for h in heads]
    cross = [_dot((q[h] * qdec_ref[h]).astype(BF16), state[h].astype(BF16)) for h in heads]
    k_dec = [(k[h] * kdec_ref[h]).T.astype(BF16) for h in heads]
    for h in heads:
        state_ref[h] = state[h] * gch_ref[h, 0:1, :] + _dot(k_dec[h], v[h])
    y = [_dot((scores[h] * dmask_ref[h]).astype(BF16), v[h]) + cross[h] for h in heads]
    for h in heads:
        gate = g_ref[0, :, cols[h]].astype(F32)
        o_ref[0, :, cols[h]] = (gate * _sigmoid(gate) * _rms(y[h], gn_ref[h])).astype(BF16)


def _retention(u, ret_norm_g):
    b, s, _ = u.shape
    c = RET_CHUNK
    nc = s // c
    h = RET_HEADS
    base = 3 * ATTN_WIDTH // RET_WIDTH

    inv = 1.0 / (ROPE_BASE ** jnp.linspace(0.0, 1.0, RET_HEAD_DIM // 2, dtype=F32))
    ang = jnp.arange(s)[:, None].astype(F32) * inv[None, :]
    cos_t = jnp.repeat(jnp.cos(ang), 2, axis=-1)
    sin_t = jnp.repeat(jnp.sin(ang), 2, axis=-1) * jnp.tile(jnp.array([-1.0, 1.0], F32), RET_HEAD_DIM // 2)
    log_g = jnp.log1p(-jnp.exp2(-5.0 - jnp.arange(h, dtype=F32)))
    i = jnp.arange(c, dtype=F32)
    rel = i[:, None] - i[None, :]
    dmask = jnp.where(rel[None] >= 0, jnp.exp(jnp.maximum(rel, 0.0)[None] * log_g[:, None, None]), 0.0)
    k_dec = jnp.exp((c - 1 - i)[None, :] * log_g[:, None])
    q_dec = jnp.exp((i + 1.0)[None, :] * log_g[:, None])
    g_chunk = jnp.exp(c * log_g)
    k_dec = jnp.broadcast_to(k_dec[:, :, None], (h, c, LANES))
    q_dec = jnp.broadcast_to(q_dec[:, :, None], (h, c, LANES))
    g_chunk = jnp.broadcast_to(g_chunk[:, None, None], (h, 8, LANES))
    gn = ret_norm_g.reshape(h, 1, RET_HEAD_DIM)

    def col(off):
        return pl.BlockSpec((1, c, RET_WIDTH), lambda bi, n: (bi, n, base + off))

    return pl.pallas_call(
        _ret_kernel,
        grid=(b, nc),
        in_specs=[
            col(0), col(1), col(2), col(3),
            pl.BlockSpec((c, LANES), lambda bi, n: (n, 0)),
            pl.BlockSpec((c, LANES), lambda bi, n: (n, 0)),
            _resident((h, c, c)), _resident((h, c, LANES)), _resident((h, c, LANES)),
            _resident((h, 8, LANES)), _resident((h, 1, RET_HEAD_DIM)),
        ],
        out_specs=pl.BlockSpec((1, c, RET_WIDTH), lambda bi, n: (bi, n, 0)),
        out_shape=jax.ShapeDtypeStruct((b, s, RET_WIDTH), BF16),
        scratch_shapes=[pltpu.VMEM((h, RET_HEAD_DIM, RET_HEAD_DIM), F32)],
        compiler_params=_params("parallel", "arbitrary"),
        name="retention",
    )(u, u, u, u, cos_t, sin_t, dmask, k_dec, q_dec, g_chunk, gn)


def _out_proj_kernel(h_ref, a_ref, r_ref, wa_ref, wr_ref, o_ref):
    o_ref[...] = h_ref[...] + _dot(a_ref[...], wa_ref[...]) + _dot(r_ref[...], wr_ref[...])


def _out_proj(h, a, r, wa, wr):
    t = h.shape[0]
    row = lambda w: pl.BlockSpec((ROW_TILE, w), lambda i: (i, 0))
    return pl.pallas_call(
        _out_proj_kernel,
        grid=(t // ROW_TILE,),
        in_specs=[row(D_MODEL), row(ATTN_WIDTH), row(RET_WIDTH),
                  _resident((ATTN_WIDTH, D_MODEL)), _resident((RET_WIDTH, D_MODEL))],
        out_specs=row(D_MODEL),
        out_shape=jax.ShapeDtypeStruct((t, D_MODEL), F32),
        compiler_params=_params("parallel"),
        name="out_proj",
    )(h, a, r, wa, wr)


def _ffn_kernel(h_ref, g_ref, wg_ref, wu_ref, wo_ref, o_ref, n_ref):
    h = h_ref[...]
    n_ref[...] = _rms(h, g_ref[...]).astype(BF16)
    o_ref[...] = h

    def body(c, carry):
        n = n_ref[...]
        zg = _dot(n, wg_ref[c])
        zu = _dot(n, wu_ref[c])
        act = (zg * _sigmoid(zg) * zu).astype(BF16)
        o_ref[...] += _dot(act, wo_ref[c])
        return carry

    lax.fori_loop(0, N_FF_CHUNKS, body, 0)


def _ffn(h, g, wg, wu, wo):
    t = h.shape[0]
    row = pl.BlockSpec((ROW_TILE, D_MODEL), lambda i: (i, 0))
    return pl.pallas_call(
        _ffn_kernel,
        grid=(t // ROW_TILE,),
        in_specs=[row, _resident((1, D_MODEL)),
                  _resident((N_FF_CHUNKS, D_MODEL, FF_CHUNK)),
                  _resident((N_FF_CHUNKS, D_MODEL, FF_CHUNK)),
                  _resident((N_FF_CHUNKS, FF_CHUNK, D_MODEL))],
        out_specs=row,
        out_shape=jax.ShapeDtypeStruct((t, D_MODEL), F32),
        scratch_shapes=[pltpu.VMEM((ROW_TILE, D_MODEL), BF16)],
        compiler_params=_params("parallel"),
        name="ffn",
    )(h, g, wg, wu, wo)


def _ple_kernel(h_ref, p_ref, g_ref, wpg_ref, wpp_ref, fg_ref, o_ref, *, final_norm):
    h = h_ref[...]
    gate = _sigmoid(_dot(_rms(h, g_ref[...]).astype(BF16), wpg_ref[...]))
    out = h + gate * _dot(p_ref[...].astype(BF16), wpp_ref[...])
    if final_norm:
        out = _rms(out, fg_ref[...])
    o_ref[...] = out


def _ple(h, p, g, wpg, wpp, fg, final_norm):
    t = h.shape[0]
    row = lambda w: pl.BlockSpec((ROW_TILE, w), lambda i: (i, 0))
    return pl.pallas_call(
        functools.partial(_ple_kernel, final_norm=final_norm),
        grid=(t // ROW_TILE,),
        in_specs=[row(D_MODEL), row(PLE_DIM), _resident((1, D_MODEL)),
                  _resident((D_MODEL, D_MODEL)), _resident((PLE_DIM, D_MODEL)),
                  _resident((1, D_MODEL))],
        out_specs=row(D_MODEL),
        out_shape=jax.ShapeDtypeStruct((t, D_MODEL), F32),
        compiler_params=_params("parallel"),
        name="ple",
    )(h, p, g, wpg, wpp, fg)


def kernel(x, p, attn_norm_g, w_in, ret_norm_g, w_out, ffn_norm_g, w_ffn_in, w_ffn_out,
           ple_norm_g, w_ple_gate, w_ple_proj, final_norm_g):
    b, s, d = x.shape
    depth = p.shape[0]
    assert d == D_MODEL and s % MOBA_BLOCK == 0 and s >= (MOBA_TOPK + 1) * MOBA_BLOCK
    assert (b * s) % ROW_TILE == 0 and s // MOBA_BLOCK <= LANES - ATTN_HEAD_DIM
    t = b * s
    h = x.reshape(t, d)
    gain = lambda g: g.reshape(1, -1).astype(F32)
    for i in range(depth):
        u = _in_proj(h, gain(attn_norm_g[i]), w_in[i].astype(BF16)).reshape(b, s, IN_WIDTH)
        a = _moba(u).reshape(t, ATTN_WIDTH)
        r = _retention(u, ret_norm_g[i].astype(F32)).reshape(t, RET_WIDTH)
        wo = w_out[i].astype(BF16)
        h = _out_proj(h, a, r, wo[:ATTN_WIDTH], wo[ATTN_WIDTH:])
        wf = w_ffn_in[i].astype(BF16)
        chunked = lambda w: w.reshape(D_MODEL, N_FF_CHUNKS, FF_CHUNK).transpose(1, 0, 2)
        h = _ffn(h, gain(ffn_norm_g[i]), chunked(wf[:, :D_FF]), chunked(wf[:, D_FF:]),
                 w_ffn_out[i].astype(BF16).reshape(N_FF_CHUNKS, FF_CHUNK, D_MODEL))
        h = _ple(h, p[i].reshape(t, PLE_DIM), gain(ple_norm_g[i]), w_ple_gate[i].astype(BF16),
                 w_ple_proj[i].astype(BF16), gain(final_norm_g), final_norm=(i == depth - 1))
    return h.reshape(b, s, d)
```

```python
import functools

import jax
import jax.numpy as jnp
from jax import lax
from jax.experimental import pallas as pl
from jax.experimental.pallas import tpu as pltpu

D_MODEL = 1024
PLE_DIM = 256
ATTN_WIDTH = 512
ATTN_HEAD_DIM = 64
RET_WIDTH = 512
RET_HEADS = 4
RET_HEAD_DIM = 128
IN_WIDTH = 3 * ATTN_WIDTH + 4 * RET_WIDTH
MOBA_BLOCK = 256
MOBA_TOPK = 3
RET_CHUNK = 256
ROPE_BASE = 10000.0
D_FF = 2816
EPS = 1e-6

LANES = 128
FF_CHUNK = 256
N_FF_CHUNKS = D_FF // FF_CHUNK
ROW_TILE = 512
IN_COL_CHUNK = 512
MASKED = -1e30
LOG2_E = 1.4426950408889634
VMEM_LIMIT = 52 * 1024 * 1024

F32 = jnp.float32
BF16 = jnp.bfloat16


def _dot(a, b):
    return jnp.dot(a, b, preferred_element_type=F32)


def _dot_nt(a, b):
    return lax.dot_general(a, b, (((1,), (1,)), ((), ())), preferred_element_type=F32)


def _rms(x, g):
    return x * lax.rsqrt(jnp.mean(x * x, axis=-1, keepdims=True) + EPS) * g


def _sigmoid(x):
    return 1.0 / (1.0 + jnp.exp(-x))


def _resident(shape):
    zeros = (0,) * len(shape)
    return pl.BlockSpec(shape, lambda *_: zeros, pipeline_mode=pl.Buffered(1))


def _params(*sem):
    return pltpu.CompilerParams(dimension_semantics=sem, vmem_limit_bytes=VMEM_LIMIT)


def _in_proj_kernel(x_ref, g_ref, w_ref, o_ref):
    n = _rms(x_ref[...], g_ref[...]).astype(BF16)
    for c in range(IN_WIDTH // IN_COL_CHUNK):
        cols = slice(c * IN_COL_CHUNK, (c + 1) * IN_COL_CHUNK)
        o_ref[:, cols] = _dot(n, w_ref[:, cols]).astype(BF16)


def _in_proj(h, g, w):
    t = h.shape[0]
    return pl.pallas_call(
        _in_proj_kernel,
        grid=(t // ROW_TILE,),
        in_specs=[
            pl.BlockSpec((ROW_TILE, D_MODEL), lambda i: (i, 0)),
            _resident((1, D_MODEL)),
            _resident((D_MODEL, IN_WIDTH)),
        ],
        out_specs=pl.BlockSpec((ROW_TILE, IN_WIDTH), lambda i: (i, 0)),
        out_shape=jax.ShapeDtypeStruct((t, IN_WIDTH), BF16),
        compiler_params=_params("parallel"),
        name="in_proj",
    )(h, g, w)


ONES_ROWS = 16
VT_ROWS = ATTN_HEAD_DIM + ONES_ROWS


def _moba_select(gate, qb):
    nb = gate.shape[0]
    row = lax.broadcasted_iota(jnp.int32, gate.shape, 0)
    past = row < qb
    g = jnp.where(past, gate, -jnp.inf)
    rank = jnp.zeros(gate.shape, jnp.int32)
    for m in range(nb):
        gm = g[m:m + 1, :]
        beats = (gm > g) | ((gm == g) & (row > m))
        rank = rank + beats.astype(jnp.int32)
    return jnp.where(past & (rank < MOBA_TOPK), 0.0, MASKED)


def _moba_kernel(q_ref, k_ref, v_ref, o_ref, kaug0_ref, kaug1_ref, vt0_ref, vt1_ref, kmean_ref,
                 sa_ref, sb_ref, *, nb):
    qb = pl.program_id(2)
    blk = MOBA_BLOCK
    hd = ATTN_HEAD_DIM

    @pl.when(qb == 0)
    def _():
        lane = lax.broadcasted_iota(jnp.int32, (blk, LANES), 1)
        head0_lane = lane < hd
        lane1 = lax.broadcasted_iota(jnp.int32, (1, LANES), 1)
        ones = jnp.ones((ONES_ROWS, blk), BF16)
        for n in range(nb):
            rows = slice(n * blk, (n + 1) * blk)
            kb = k_ref[0, rows, :]
            mean = jnp.sum(kb.astype(F32), axis=0, keepdims=True) * (1.0 / blk)
            kmean_ref[n:n + 1, :] = jnp.where(lane1 < hd, mean, 0.0)
            kmean_ref[nb + n:nb + n + 1, :] = jnp.where(lane1 < hd, 0.0, mean)
            kaug0_ref[n] = jnp.where(head0_lane, kb, (lane == hd + n).astype(BF16))
            kaug1_ref[n] = jnp.where(head0_lane, (lane == n).astype(BF16), kb)
            vt = v_ref[0, rows, :].astype(F32).T.astype(BF16)
            vt0_ref[n] = jnp.concatenate([vt[:hd], ones], axis=0)
            vt1_ref[n] = jnp.concatenate([vt[hd:], ones], axis=0)

    qt = (q_ref[0].astype(F32) * (hd ** -0.5 * LOG2_E)).T.astype(BF16)
    km = kmean_ref[...]
    km_hi = km.astype(BF16)
    km_lo = (km - km_hi.astype(F32)).astype(BF16)
    gate = _dot(km_hi, qt) + _dot(km_lo, qt)
    mask0 = _moba_select(gate[:nb], qb).astype(BF16)
    mask1 = _moba_select(gate[nb:], qb).astype(BF16)
    zeros = lambda r: jnp.zeros((r, blk), BF16)
    qt_past0 = jnp.concatenate([qt[:hd], mask0, zeros(LANES - hd - nb)], axis=0)
    qt_past1 = jnp.concatenate([mask1, zeros(hd - nb), qt[hd:]], axis=0)
    qt_own0 = jnp.concatenate([qt[:hd], zeros(LANES - hd)], axis=0)
    qt_own1 = jnp.concatenate([zeros(hd), qt[hd:]], axis=0)

    def partials(logits, vts):
        maxes = [jnp.max(s, axis=0, keepdims=True) for s in logits]
        probs = [jnp.exp2(s - m).astype(BF16) for s, m in zip(logits, maxes)]
        return [(m, _dot(vt, p)) for m, vt, p in zip(maxes, vts, probs)]

    def merge(m, acc, parts):
        m_new = m
        for mp, _ in parts:
            m_new = jnp.maximum(m_new, mp)
        acc = acc * jnp.exp2(m - m_new)
        for mp, rp in parts:
            acc = acc + rp * jnp.exp2(mp - m_new)
        return m_new, acc

    def issue_logits(s_ref, pair):
        for j in range(2):
            n = jnp.minimum(2 * pair + j, nb - 1)
            s_ref[2 * j] = _dot(kaug0_ref[n], qt_past0)
            s_ref[2 * j + 1] = _dot(kaug1_ref[n], qt_past1)

    def consume(s_ref, pair, carry):
        m0, a0, m1, a1 = carry
        blocks = [jnp.minimum(2 * pair + j, nb - 1) for j in range(2)]
        vts = [vt0_ref[blocks[0]], vt1_ref[blocks[0]], vt0_ref[blocks[1]], vt1_ref[blocks[1]]]
        parts = partials([s_ref[c] for c in range(4)], vts)
        return merge(m0, a0, parts[0::2]) + merge(m1, a1, parts[1::2])

    issue_logits(sa_ref, 0)

    kpos = lax.broadcasted_iota(jnp.int32, (blk, blk), 0)
    qpos = lax.broadcasted_iota(jnp.int32, (blk, blk), 1)
    causal = kpos <= qpos
    own_logits = [jnp.where(causal, _dot(kaug0_ref[qb], qt_own0), MASKED),
                  jnp.where(causal, _dot(kaug1_ref[qb], qt_own1), MASKED)]
    own0, own1 = partials(own_logits, [vt0_ref[qb], vt1_ref[qb]])

    def body(t, carry):
        issue_logits(sb_ref, 2 * t + 1)
        carry = consume(sa_ref, 2 * t, carry)
        issue_logits(sa_ref, 2 * t + 2)
        return consume(sb_ref, 2 * t + 1, carry)

    n_pairs = (qb + 1) // 2
    _, a0, _, a1 = lax.fori_loop(0, (n_pairs + 1) // 2, body, own0 + own1)
    out_t = jnp.concatenate([a0[:hd] / a0[hd:hd + 1], a1[:hd] / a1[hd:hd + 1]], axis=0)
    o_ref[0] = out_t.T.astype(BF16)


def _moba(u):
    b, s, _ = u.shape
    nb = s // MOBA_BLOCK
    assert nb % ONES_ROWS == 0 and nb <= ATTN_HEAD_DIM
    pairs = ATTN_WIDTH // LANES
    return pl.pallas_call(
        functools.partial(_moba_kernel, nb=nb),
        grid=(b, pairs, nb),
        in_specs=[
            pl.BlockSpec((1, MOBA_BLOCK, LANES), lambda bi, hp, qb: (bi, qb, hp)),
            pl.BlockSpec((1, s, LANES), lambda bi, hp, qb: (bi, 0, pairs + hp)),
            pl.BlockSpec((1, s, LANES), lambda bi, hp, qb: (bi, 0, 2 * pairs + hp)),
        ],
        out_specs=pl.BlockSpec((1, MOBA_BLOCK, LANES), lambda bi, hp, qb: (bi, qb, hp)),
        out_shape=jax.ShapeDtypeStruct((b, s, ATTN_WIDTH), BF16),
        scratch_shapes=[
            pltpu.VMEM((nb, MOBA_BLOCK, LANES), BF16),
            pltpu.VMEM((nb, MOBA_BLOCK, LANES), BF16),
            pltpu.VMEM((nb, VT_ROWS, MOBA_BLOCK), BF16),
            pltpu.VMEM((nb, VT_ROWS, MOBA_BLOCK), BF16),
            pltpu.VMEM((2 * nb, LANES), F32),
            pltpu.VMEM((4, MOBA_BLOCK, MOBA_BLOCK), F32),
            pltpu.VMEM((4, MOBA_BLOCK, MOBA_BLOCK), F32),
        ],
        compiler_params=_params("parallel", "parallel", "arbitrary"),
        name="moba",
    )(u, u, u)


def _ret_kernel(q_ref, k_ref, v_ref, g_ref, cos_ref, sin_ref, dmask_ref, kdec_ref, qdec_ref,
                gch_ref, gn_ref, o_ref, state_ref):
    @pl.when(pl.program_id(1) == 0)
    def _():
        state_ref[...] = jnp.zeros_like(state_ref)

    cos = cos_ref[...]
    sin = sin_ref[...]
    lane = lax.broadcasted_iota(jnp.int32, cos.shape, 1)
    even = (lane & 1) == 0

    def rot(x):
        partner = jnp.where(even, pltpu.roll(x, LANES - 1, 1), pltpu.roll(x, 1, 1))
        return x * cos + partner * sin

    heads = range(RET_HEADS)
    cols = [slice(h * RET_HEAD_DIM, (h + 1) * RET_HEAD_DIM) for h in heads]
    q = [rot(q_ref[0, :, c].astype(F32)) for c in cols]
    k = [rot(k_ref[0, :, c].astype(F32)) * (RET_HEAD_DIM ** -0.5) for c in cols]
    v = [v_ref[0, :, c] for c in cols]
    state = [state_ref[h] for h in heads]
    scores = [_dot_nt(q[h].astype(BF16), k[h].astype(BF16)) for h in heads]
    cross = [_dot((q[h] * qdec_ref[h]).astype(BF16), state[h].astype(BF16)) for h in heads]
    k_dec = [(k[h] * kdec_ref[h]).T.astype(BF16) for h in heads]
    for h in heads:
        state_ref[h] = state[h] * gch_ref[h, 0:1, :] + _dot(k_dec[h], v[h])
    y = [_dot((scores[h] * dmask_ref[h]).astype(BF16), v[h]) + cross[h] for h in heads]
    for h in heads:
        gate = g_ref[0, :, cols[h]].astype(F32)
        o_ref[0, :, cols[h]] = (gate * _sigmoid(gate) * _rms(y[h], gn_ref[h])).astype(BF16)


def _retention(u, ret_norm_g):
    b, s, _ = u.shape
    c = RET_CHUNK
    nc = s // c
    h = RET_HEADS
    base = 3 * ATTN_WIDTH // RET_WIDTH

    inv = 1.0 / (ROPE_BASE ** jnp.linspace(0.0, 1.0, RET_HEAD_DIM // 2, dtype=F32))
    ang = jnp.arange(s)[:, None].astype(F32) * inv[None, :]
    cos_t = jnp.repeat(jnp.cos(ang), 2, axis=-1)
    sin_t = jnp.repeat(jnp.sin(ang), 2, axis=-1) * jnp.tile(jnp.array([-1.0, 1.0], F32), RET_HEAD_DIM // 2)
    log_g = jnp.log1p(-jnp.exp2(-5.0 - jnp.arange(h, dtype=F32)))
    i = jnp.arange(c, dtype=F32)
    rel = i[:, None] - i[None, :]
    dmask = jnp.where(rel[None] >= 0, jnp.exp(jnp.maximum(rel, 0.0)[None] * log_g[:, None, None]), 0.0)
    k_dec = jnp.exp((c - 1 - i)[None, :] * log_g[:, None])
    q_dec = jnp.exp((i + 1.0)[None, :] * log_g[:, None])
    g_chunk = jnp.exp(c * log_g)
    k_dec = jnp.broadcast_to(k_dec[:, :, None], (h, c, LANES))
    q_dec = jnp.broadcast_to(q_dec[:, :, None], (h, c, LANES))
    g_chunk = jnp.broadcast_to(g_chunk[:, None, None], (h, 8, LANES))
    gn = ret_norm_g.reshape(h, 1, RET_HEAD_DIM)

    def col(off):
        return pl.BlockSpec((1, c, RET_WIDTH), lambda bi, n: (bi, n, base + off))

    return pl.pallas_call(
        _ret_kernel,
        grid=(b, nc),
        in_specs=[
            col(0), col(1), col(2), col(3),
            pl.BlockSpec((c, LANES), lambda bi, n: (n, 0)),
            pl.BlockSpec((c, LANES), lambda bi, n: (n, 0)),
            _resident((h, c, c)), _resident((h, c, LANES)), _resident((h, c, LANES)),
            _resident((h, 8, LANES)), _resident((h, 1, RET_HEAD_DIM)),
        ],
        out_specs=pl.BlockSpec((1, c, RET_WIDTH), lambda bi, n: (bi, n, 0)),
        out_shape=jax.ShapeDtypeStruct((b, s, RET_WIDTH), BF16),
        scratch_shapes=[pltpu.VMEM((h, RET_HEAD_DIM, RET_HEAD_DIM), F32)],
        compiler_params=_params("parallel", "arbitrary"),
        name="retention",
    )(u, u, u, u, cos_t, sin_t, dmask, k_dec, q_dec, g_chunk, gn)


def _tail_kernel(h_ref, a_ref, r_ref, p_ref, wa_ref, wr_ref, g_ffn_ref, w_in_ref, w_out_ref,
                 g_ple_ref, wpg_ref, wpp_ref, g_final_ref, o_ref, act_ref, *, final_norm):
    h1 = h_ref[...] + _dot(a_ref[...], wa_ref[...]) + _dot(r_ref[...], wr_ref[...])
    n1 = _rms(h1, g_ffn_ref[...]).astype(BF16)
    proj = _dot(p_ref[...].astype(BF16), wpp_ref[...])
    for c in range(N_FF_CHUNKS):
        zg = _dot(n1, w_in_ref[:, c * FF_CHUNK:(c + 1) * FF_CHUNK])
        zu = _dot(n1, w_in_ref[:, D_FF + c * FF_CHUNK:D_FF + (c + 1) * FF_CHUNK])
        act_ref[:, c * FF_CHUNK:(c + 1) * FF_CHUNK] = (zg * _sigmoid(zg) * zu).astype(BF16)
    h2 = h1 + _dot(act_ref[...], w_out_ref[...])
    gate = _sigmoid(_dot(_rms(h2, g_ple_ref[...]).astype(BF16), wpg_ref[...]))
    out = h2 + gate * proj
    if final_norm:
        out = _rms(out, g_final_ref[...])
    o_ref[...] = out


def _tail(h, a, r, p, wa, wr, g_ffn, w_ffn_in, w_ffn_out, g_ple, wpg, wpp, g_final, final_norm):
    t = h.shape[0]
    row = lambda w: pl.BlockSpec((ROW_TILE, w), lambda i: (i, 0))
    return pl.pallas_call(
        functools.partial(_tail_kernel, final_norm=final_norm),
        grid=(t // ROW_TILE,),
        in_specs=[row(D_MODEL), row(ATTN_WIDTH), row(RET_WIDTH), row(PLE_DIM),
                  _resident((ATTN_WIDTH, D_MODEL)), _resident((RET_WIDTH, D_MODEL)),
                  _resident((1, D_MODEL)), _resident((D_MODEL, 2 * D_FF)),
                  _resident((D_FF, D_MODEL)), _resident((1, D_MODEL)),
                  _resident((D_MODEL, D_MODEL)), _resident((PLE_DIM, D_MODEL)),
                  _resident((1, D_MODEL))],
        out_specs=row(D_MODEL),
        out_shape=jax.ShapeDtypeStruct((t, D_MODEL), F32),
        scratch_shapes=[pltpu.VMEM((ROW_TILE, D_FF), BF16)],
        compiler_params=_params("parallel"),
        name="tail",
    )(h, a, r, p, wa, wr, g_ffn, w_ffn_in, w_ffn_out, g_ple, wpg, wpp, g_final)


def kernel(x, p, attn_norm_g, w_in, ret_norm_g, w_out, ffn_norm_g, w_ffn_in, w_ffn_out,
           ple_norm_g, w_ple_gate, w_ple_proj, final_norm_g):
    b, s, d = x.shape
    depth = p.shape[0]
    assert d == D_MODEL and s % MOBA_BLOCK == 0 and s >= (MOBA_TOPK + 1) * MOBA_BLOCK
    assert (b * s) % ROW_TILE == 0 and s // MOBA_BLOCK <= LANES - ATTN_HEAD_DIM
    t = b * s
    h = x.reshape(t, d)
    gain = lambda g: g.reshape(1, -1).astype(F32)
    for i in range(depth):
        u = _in_proj(h, gain(attn_norm_g[i]), w_in[i].astype(BF16)).reshape(b, s, IN_WIDTH)
        a = _moba(u).reshape(t, ATTN_WIDTH)
        r = _retention(u, ret_norm_g[i].astype(F32)).reshape(t, RET_WIDTH)
        wo = w_out[i].astype(BF16)
        h = _tail(h, a, r, p[i].reshape(t, PLE_DIM), wo[:ATTN_WIDTH], wo[ATTN_WIDTH:],
                  gain(ffn_norm_g[i]), w_ffn_in[i].astype(BF16), w_ffn_out[i].astype(BF16),
                  gain(ple_norm_g[i]), w_ple_gate[i].astype(BF16), w_ple_proj[i].astype(BF16),
                  gain(final_norm_g), final_norm=(i == depth - 1))
    return h.reshape(b, s, d)
```

```python
import functools

import jax
import jax.numpy as jnp
from jax import lax
from jax.experimental import pallas as pl
from jax.experimental.pallas import tpu as pltpu

D_MODEL = 1024
PLE_DIM = 256
ATTN_WIDTH = 512
ATTN_HEAD_DIM = 64
RET_WIDTH = 512
RET_HEADS = 4
RET_HEAD_DIM = 128
IN_WIDTH = 3 * ATTN_WIDTH + 4 * RET_WIDTH
MOBA_BLOCK = 256
MOBA_TOPK = 3
RET_CHUNK = 256
ROPE_BASE = 10000.0
D_FF = 2816
EPS = 1e-6

LANES = 128
FF_CHUNK = 256
N_FF_CHUNKS = D_FF // FF_CHUNK
ROW_TILE = 512
IN_COL_CHUNK = 512
MASKED = -1e30
LOG2_E = 1.4426950408889634
VMEM_LIMIT = 52 * 1024 * 1024

F32 = jnp.float32
BF16 = jnp.bfloat16


def _dot(a, b):
    return jnp.dot(a, b, preferred_element_type=F32)


def _dot_nt(a, b):
    return lax.dot_general(a, b, (((1,), (1,)), ((), ())), preferred_element_type=F32)


def _rms(x, g):
    return x * lax.rsqrt(jnp.mean(x * x, axis=-1, keepdims=True) + EPS) * g


def _sigmoid(x):
    return 1.0 / (1.0 + jnp.exp(-x))


def _resident(shape):
    zeros = (0,) * len(shape)
    return pl.BlockSpec(shape, lambda *_: zeros, pipeline_mode=pl.Buffered(1))


def _params(*sem):
    return pltpu.CompilerParams(dimension_semantics=sem, vmem_limit_bytes=VMEM_LIMIT)


def _in_proj_kernel(x_ref, g_ref, w_ref, o_ref):
    n = _rms(x_ref[...], g_ref[...]).astype(BF16)
    for c in range(IN_WIDTH // IN_COL_CHUNK):
        cols = slice(c * IN_COL_CHUNK, (c + 1) * IN_COL_CHUNK)
        o_ref[:, cols] = _dot(n, w_ref[:, cols]).astype(BF16)


def _in_proj(h, g, w):
    t = h.shape[0]
    return pl.pallas_call(
        _in_proj_kernel,
        grid=(t // ROW_TILE,),
        in_specs=[
            pl.BlockSpec((ROW_TILE, D_MODEL), lambda i: (i, 0)),
            _resident((1, D_MODEL)),
            _resident((D_MODEL, IN_WIDTH)),
        ],
        out_specs=pl.BlockSpec((ROW_TILE, IN_WIDTH), lambda i: (i, 0)),
        out_shape=jax.ShapeDtypeStruct((t, IN_WIDTH), BF16),
        compiler_params=_params("parallel"),
        name="in_proj",
    )(h, g, w)


ONES_ROWS = 16
VT_ROWS = ATTN_HEAD_DIM + ONES_ROWS
Q_TILE = 2 * MOBA_BLOCK


def _moba_select(gate, q_block):
    nb = gate.shape[0]
    row = lax.broadcasted_iota(jnp.int32, gate.shape, 0)
    past = row < q_block
    g = jnp.where(past, gate, -jnp.inf)
    rank = jnp.zeros(gate.shape, jnp.int32)
    for m in range(nb):
        gm = g[m:m + 1, :]
        beats = (gm > g) | ((gm == g) & (row > m))
        rank = rank + beats.astype(jnp.int32)
    return jnp.where((past & (rank < MOBA_TOPK)) | (row == q_block), 0.0, MASKED)


def _moba_kernel(q_ref, k_ref, v_ref, o_ref, kaug0_ref, kaug1_ref, vt0_ref, vt1_ref, kmean_ref,
                 sa_ref, sb_ref, *, nb):
    t = pl.program_id(2)
    blk = MOBA_BLOCK
    hd = ATTN_HEAD_DIM

    @pl.when(t == 0)
    def _():
        lane = lax.broadcasted_iota(jnp.int32, (blk, LANES), 1)
        head0_lane = lane < hd
        lane1 = lax.broadcasted_iota(jnp.int32, (1, LANES), 1)
        ones = jnp.ones((ONES_ROWS, blk), BF16)
        for n in range(nb):
            rows = slice(n * blk, (n + 1) * blk)
            kb = k_ref[0, rows, :]
            mean = jnp.sum(kb.astype(F32), axis=0, keepdims=True) * (1.0 / blk)
            kmean_ref[n:n + 1, :] = jnp.where(lane1 < hd, mean, 0.0)
            kmean_ref[nb + n:nb + n + 1, :] = jnp.where(lane1 < hd, 0.0, mean)
            kaug0_ref[n] = jnp.where(head0_lane, kb, (lane == hd + n).astype(BF16))
            kaug1_ref[n] = jnp.where(head0_lane, (lane == n).astype(BF16), kb)
            vt = v_ref[0, rows, :].astype(F32).T.astype(BF16)
            vt0_ref[n] = jnp.concatenate([vt[:hd], ones], axis=0)
            vt1_ref[n] = jnp.concatenate([vt[hd:], ones], axis=0)

    qt = (q_ref[0].astype(F32) * (hd ** -0.5 * LOG2_E)).T.astype(BF16)
    km = kmean_ref[...]
    km_hi = km.astype(BF16)
    km_lo = (km - km_hi.astype(F32)).astype(BF16)
    gate = _dot(km_hi, qt) + _dot(km_lo, qt)
    first = 2 * t
    q_block = first + (lax.broadcasted_iota(jnp.int32, (1, Q_TILE), 1) >= blk).astype(jnp.int32)
    mask0 = _moba_select(gate[:nb], q_block).astype(BF16)
    mask1 = _moba_select(gate[nb:], q_block).astype(BF16)
    zeros = lambda r: jnp.zeros((r, Q_TILE), BF16)
    qt0 = jnp.concatenate([qt[:hd], mask0, zeros(LANES - hd - nb)], axis=0)
    qt1 = jnp.concatenate([mask1, zeros(hd - nb), qt[hd:]], axis=0)

    kpos = lax.broadcasted_iota(jnp.int32, (blk, blk), 0)
    qpos = lax.broadcasted_iota(jnp.int32, (blk, blk), 1)
    causal = kpos <= qpos

    def issue_logits(s_ref, n, causal_half=None):
        for head, (kaug_ref, qt_aug) in enumerate(((kaug0_ref, qt0), (kaug1_ref, qt1))):
            s = _dot(kaug_ref[n], qt_aug)
            if causal_half is None:
                s_ref[head] = s
            else:
                cols = slice(causal_half * blk, (causal_half + 1) * blk)
                rest = slice((1 - causal_half) * blk, (2 - causal_half) * blk)
                s_ref[head, :, cols] = jnp.where(causal, s[:, cols], MASKED)
                s_ref[head, :, rest] = s[:, rest]

    def consume(s_ref, n, carry):
        out = ()
        for head, vt_ref in enumerate((vt0_ref, vt1_ref)):
            m, acc = carry[2 * head], carry[2 * head + 1]
            s = s_ref[head]
            m_blk = jnp.max(s, axis=0, keepdims=True)
            m_new = jnp.maximum(m, m_blk)
            part = _dot(vt_ref[n], jnp.exp2(s - m_blk).astype(BF16))
            out += (m_new, acc * jnp.exp2(m - m_new) + part * jnp.exp2(m_blk - m_new))
        return out

    m_init = jnp.full((1, Q_TILE), 2 * MASKED, F32)
    a_init = jnp.zeros((VT_ROWS, Q_TILE), F32)
    issue_logits(sa_ref, first, causal_half=0)
    issue_logits(sb_ref, first + 1, causal_half=1)
    carry = consume(sa_ref, first, (m_init, a_init, m_init, a_init))
    issue_logits(sa_ref, 0)
    carry = consume(sb_ref, first + 1, carry)

    def body(i, carry):
        issue_logits(sb_ref, 2 * i - 1)
        carry = consume(sa_ref, 2 * i - 2, carry)
        issue_logits(sa_ref, 2 * i)
        return consume(sb_ref, 2 * i - 1, carry)

    _, a0, _, a1 = lax.fori_loop(1, t + 1, body, carry)
    out_t = jnp.concatenate([a0[:hd] / a0[hd:hd + 1], a1[:hd] / a1[hd:hd + 1]], axis=0)
    o_ref[0] = out_t.T.astype(BF16)


def _moba(u):
    b, s, _ = u.shape
    nb = s // MOBA_BLOCK
    assert nb % ONES_ROWS == 0 and nb <= ATTN_HEAD_DIM and s % Q_TILE == 0
    pairs = ATTN_WIDTH // LANES
    return pl.pallas_call(
        functools.partial(_moba_kernel, nb=nb),
        grid=(b, pairs, s // Q_TILE),
        in_specs=[
            pl.BlockSpec((1, Q_TILE, LANES), lambda bi, hp, t: (bi, t, hp)),
            pl.BlockSpec((1, s, LANES), lambda bi, hp, t: (bi, 0, pairs + hp)),
            pl.BlockSpec((1, s, LANES), lambda bi, hp, t: (bi, 0, 2 * pairs + hp)),
        ],
        out_specs=pl.BlockSpec((1, Q_TILE, LANES), lambda bi, hp, t: (bi, t, hp)),
        out_shape=jax.ShapeDtypeStruct((b, s, ATTN_WIDTH), BF16),
        scratch_shapes=[
            pltpu.VMEM((nb, MOBA_BLOCK, LANES), BF16),
            pltpu.VMEM((nb, MOBA_BLOCK, LANES), BF16),
            pltpu.VMEM((nb, VT_ROWS, MOBA_BLOCK), BF16),
            pltpu.VMEM((nb, VT_ROWS, MOBA_BLOCK), BF16),
            pltpu.VMEM((2 * nb, LANES), F32),
            pltpu.VMEM((2, MOBA_BLOCK, Q_TILE), F32),
            pltpu.VMEM((2, MOBA_BLOCK, Q_TILE), F32),
        ],
        compiler_params=_params("parallel", "parallel", "arbitrary"),
        name="moba",
    )(u, u, u)


def _ret_kernel(q_ref, k_ref, v_ref, g_ref, cos_ref, sin_ref, dmask_ref, kdec_ref, qdec_ref,
                gch_ref, gn_ref, o_ref, state_ref):
    @pl.when(pl.program_id(1) == 0)
    def _():
        state_ref[...] = jnp.zeros_like(state_ref)

    cos = cos_ref[...]
    sin = sin_ref[...]
    lane = lax.broadcasted_iota(jnp.int32, cos.shape, 1)
    even = (lane & 1) == 0

    def rot(x):
        partner = jnp.where(even, pltpu.roll(x, LANES - 1, 1), pltpu.roll(x, 1, 1))
        return x * cos + partner * sin

    heads = range(RET_HEADS)
    cols = [slice(h * RET_HEAD_DIM, (h + 1) * RET_HEAD_DIM) for h in heads]
    q = [rot(q_ref[0, :, c].astype(F32)) for c in cols]
    k = [rot(k_ref[0, :, c].astype(F32)) * (RET_HEAD_DIM ** -0.5) for c in cols]
    v = [v_ref[0, :, c] for c in cols]
    state = [state_ref[h] for h in heads]
    scores = [_dot_nt(q[h].astype(BF16), k[h].astype(BF16)) for h in heads]
    cross = [_dot((q[h] * qdec_ref[h]).astype(BF16), state[h].astype(BF16)) for h in heads]
    k_dec = [(k[h] * kdec_ref[h]).T.astype(BF16) for h in heads]
    for h in heads:
        state_ref[h] = state[h] * gch_ref[h, 0:1, :] + _dot(k_dec[h], v[h])
    y = [_dot((scores[h] * dmask_ref[h]).astype(BF16), v[h]) + cross[h] for h in heads]
    for h in heads:
        gate = g_ref[0, :, cols[h]].astype(F32)
        o_ref[0, :, cols[h]] = (gate * _sigmoid(gate) * _rms(y[h], gn_ref[h])).astype(BF16)


def _retention(u, ret_norm_g):
    b, s, _ = u.shape
    c = RET_CHUNK
    nc = s // c
    h = RET_HEADS
    base = 3 * ATTN_WIDTH // RET_WIDTH

    inv = 1.0 / (ROPE_BASE ** jnp.linspace(0.0, 1.0, RET_HEAD_DIM // 2, dtype=F32))
    ang = jnp.arange(s)[:, None].astype(F32) * inv[None, :]
    cos_t = jnp.repeat(jnp.cos(ang), 2, axis=-1)
    sin_t = jnp.repeat(jnp.sin(ang), 2, axis=-1) * jnp.tile(jnp.array([-1.0, 1.0], F32), RET_HEAD_DIM // 2)
    log_g = jnp.log1p(-jnp.exp2(-5.0 - jnp.arange(h, dtype=F32)))
    i = jnp.arange(c, dtype=F32)
    rel = i[:, None] - i[None, :]
    dmask = jnp.where(rel[None] >= 0, jnp.exp(jnp.maximum(rel, 0.0)[None] * log_g[:, None, None]), 0.0)
    k_dec = jnp.exp((c - 1 - i)[None, :] * log_g[:, None])
    q_dec = jnp.exp((i + 1.0)[None, :] * log_g[:, None])
    g_chunk = jnp.exp(c * log_g)
    k_dec = jnp.broadcast_to(k_dec[:, :, None], (h, c, LANES))
    q_dec = jnp.broadcast_to(q_dec[:, :, None], (h, c, LANES))
    g_chunk = jnp.broadcast_to(g_chunk[:, None, None], (h, 8, LANES))
    gn = ret_norm_g.reshape(h, 1, RET_HEAD_DIM)

    def col(off):
        return pl.BlockSpec((1, c, RET_WIDTH), lambda bi, n: (bi, n, base + off))

    return pl.pallas_call(
        _ret_kernel,
        grid=(b, nc),
        in_specs=[
            col(0), col(1), col(2), col(3),
            pl.BlockSpec((c, LANES), lambda bi, n: (n, 0)),
            pl.BlockSpec((c, LANES), lambda bi, n: (n, 0)),
            _resident((h, c, c)), _resident((h, c, LANES)), _resident((h, c, LANES)),
            _resident((h, 8, LANES)), _resident((h, 1, RET_HEAD_DIM)),
        ],
        out_specs=pl.BlockSpec((1, c, RET_WIDTH), lambda bi, n: (bi, n, 0)),
        out_shape=jax.ShapeDtypeStruct((b, s, RET_WIDTH), BF16),
        scratch_shapes=[pltpu.VMEM((h, RET_HEAD_DIM, RET_HEAD_DIM), F32)],
        compiler_params=_params("parallel", "arbitrary"),
        name="retention",
    )(u, u, u, u, cos_t, sin_t, dmask, k_dec, q_dec, g_chunk, gn)


def _tail_kernel(h_ref, a_ref, r_ref, p_ref, wa_ref, wr_ref, g_ffn_ref, w_in_ref, w_out_ref,
                 g_ple_ref, wpg_ref, wpp_ref, g_final_ref, o_ref, act_ref, *, final_norm):
    h1 = h_ref[...] + _dot(a_ref[...], wa_ref[...]) + _dot(r_ref[...], wr_ref[...])
    n1 = _rms(h1, g_ffn_ref[...]).astype(BF16)
    proj = _dot(p_ref[...].astype(BF16), wpp_ref[...])
    for c in range(N_FF_CHUNKS):
        zg = _dot(n1, w_in_ref[:, c * FF_CHUNK:(c + 1) * FF_CHUNK])
        zu = _dot(n1, w_in_ref[:, D_FF + c * FF_CHUNK:D_FF + (c + 1) * FF_CHUNK])
        act_ref[:, c * FF_CHUNK:(c + 1) * FF_CHUNK] = (zg * _sigmoid(zg) * zu).astype(BF16)
    h2 = h1 + _dot(act_ref[...], w_out_ref[...])
    gate = _sigmoid(_dot(_rms(h2, g_ple_ref[...]).astype(BF16), wpg_ref[...]))
    out = h2 + gate * proj
    if final_norm:
        out = _rms(out, g_final_ref[...])
    o_ref[...] = out


def _tail(h, a, r, p, wa, wr, g_ffn, w_ffn_in, w_ffn_out, g_ple, wpg, wpp, g_final, final_norm):
    t = h.shape[0]
    row = lambda w: pl.BlockSpec((ROW_TILE, w), lambda i: (i, 0))
    return pl.pallas_call(
        functools.partial(_tail_kernel, final_norm=final_norm),
        grid=(t // ROW_TILE,),
        in_specs=[row(D_MODEL), row(ATTN_WIDTH), row(RET_WIDTH), row(PLE_DIM),
                  _resident((ATTN_WIDTH, D_MODEL)), _resident((RET_WIDTH, D_MODEL)),
                  _resident((1, D_MODEL)), _resident((D_MODEL, 2 * D_FF)),
                  _resident((D_FF, D_MODEL)), _resident((1, D_MODEL)),
                  _resident((D_MODEL, D_MODEL)), _resident((PLE_DIM, D_MODEL)),
                  _resident((1, D_MODEL))],
        out_specs=row(D_MODEL),
        out_shape=jax.ShapeDtypeStruct((t, D_MODEL), F32),
        scratch_shapes=[pltpu.VMEM((ROW_TILE, D_FF), BF16)],
        compiler_params=_params("parallel"),
        name="tail",
    )(h, a, r, p, wa, wr, g_ffn, w_ffn_in, w_ffn_out, g_ple, wpg, wpp, g_final)


def kernel(x, p, attn_norm_g, w_in, ret_norm_g, w_out, ffn_norm_g, w_ffn_in, w_ffn_out,
           ple_norm_g, w_ple_gate, w_ple_proj, final_norm_g):
    b, s, d = x.shape
    depth = p.shape[0]
    assert d == D_MODEL and s % MOBA_BLOCK == 0 and s >= (MOBA_TOPK + 1) * MOBA_BLOCK
    assert (b * s) % ROW_TILE == 0 and s // MOBA_BLOCK <= LANES - ATTN_HEAD_DIM
    t = b * s
    h = x.reshape(t, d)
    gain = lambda g: g.reshape(1, -1).astype(F32)
    for i in range(depth):
        u = _in_proj(h, gain(attn_norm_g[i]), w_in[i].astype(BF16)).reshape(b, s, IN_WIDTH)
        a = _moba(u).reshape(t, ATTN_WIDTH)
        r = _retention(u, ret_norm_g[i].astype(F32)).reshape(t, RET_WIDTH)
        wo = w_out[i].astype(BF16)
        h = _tail(h, a, r, p[i].reshape(t, PLE_DIM), wo[:ATTN_WIDTH], wo[ATTN_WIDTH:],
                  gain(ffn_norm_g[i]), w_ffn_in[i].astype(BF16), w_ffn_out[i].astype(BF16),
                  gain(ple_norm_g[i]), w_ple_gate[i].astype(BF16), w_ple_proj[i].astype(BF16),
                  gain(final_norm_g), final_norm=(i == depth - 1))
    return h.reshape(b, s, d)
```

```python
import functools

import jax
import jax.numpy as jnp
from jax import lax
from jax.experimental import pallas as pl
from jax.experimental.pallas import tpu as pltpu

D_MODEL = 1024
PLE_DIM = 256
ATTN_WIDTH = 512
ATTN_HEAD_DIM = 64
RET_WIDTH = 512
RET_HEADS = 4
RET_HEAD_DIM = 128
IN_WIDTH = 3 * ATTN_WIDTH + 4 * RET_WIDTH
MOBA_BLOCK = 256
MOBA_TOPK = 3
RET_CHUNK = 256
ROPE_BASE = 10000.0
D_FF = 2816
EPS = 1e-6

LANES = 128
FF_CHUNK = 256
N_FF_CHUNKS = D_FF // FF_CHUNK
ROW_TILE = 512
IN_COL_CHUNK = 512
MASKED = -1e30
LOG2_E = 1.4426950408889634
VMEM_LIMIT = 52 * 1024 * 1024

F32 = jnp.float32
BF16 = jnp.bfloat16


def _dot(a, b):
    return jnp.dot(a, b, preferred_element_type=F32)


def _dot_nt(a, b):
    return lax.dot_general(a, b, (((1,), (1,)), ((), ())), preferred_element_type=F32)


def _rms(x, g):
    return x * lax.rsqrt(jnp.mean(x * x, axis=-1, keepdims=True) + EPS) * g


def _sigmoid(x):
    return 1.0 / (1.0 + jnp.exp(-x))


def _resident(shape):
    zeros = (0,) * len(shape)
    return pl.BlockSpec(shape, lambda *_: zeros, pipeline_mode=pl.Buffered(1))


def _params(*sem):
    return pltpu.CompilerParams(dimension_semantics=sem, vmem_limit_bytes=VMEM_LIMIT)


def _in_proj_kernel(x_ref, g_ref, w_ref, o_ref):
    n = _rms(x_ref[...], g_ref[...]).astype(BF16)
    for c in range(IN_WIDTH // IN_COL_CHUNK):
        cols = slice(c * IN_COL_CHUNK, (c + 1) * IN_COL_CHUNK)
        o_ref[:, cols] = _dot(n, w_ref[:, cols]).astype(BF16)


def _in_proj(h, g, w):
    t = h.shape[0]
    return pl.pallas_call(
        _in_proj_kernel,
        grid=(t // ROW_TILE,),
        in_specs=[
            pl.BlockSpec((ROW_TILE, D_MODEL), lambda i: (i, 0)),
            _resident((1, D_MODEL)),
            _resident((D_MODEL, IN_WIDTH)),
        ],
        out_specs=pl.BlockSpec((ROW_TILE, IN_WIDTH), lambda i: (i, 0)),
        out_shape=jax.ShapeDtypeStruct((t, IN_WIDTH), BF16),
        compiler_params=_params("parallel"),
        name="in_proj",
    )(h, g, w)


ONES_ROWS = 16
VT_ROWS = ATTN_HEAD_DIM + ONES_ROWS
Q_TILE = 2 * MOBA_BLOCK


def _moba_select(gate, q_block):
    nb = gate.shape[0]
    row = lax.broadcasted_iota(jnp.int32, gate.shape, 0)
    past = row < q_block
    g = jnp.where(past, gate, -jnp.inf)
    chosen = row == q_block
    for _ in range(MOBA_TOPK):
        top = jnp.max(g, axis=0, keepdims=True)
        hit = row == jnp.min(jnp.where(g == top, row, nb), axis=0, keepdims=True)
        chosen = chosen | (hit & past)
        g = jnp.where(hit, -jnp.inf, g)
    return jnp.where(chosen, 0.0, MASKED)


def _moba_kernel(q_ref, k_ref, v_ref, o_ref, kaug0_ref, kaug1_ref, vt0_ref, vt1_ref, kmean_ref,
                 sa_ref, sb_ref, *, nb):
    t = pl.program_id(2)
    blk = MOBA_BLOCK
    hd = ATTN_HEAD_DIM

    @pl.when(t == 0)
    def _():
        lane = lax.broadcasted_iota(jnp.int32, (blk, LANES), 1)
        head0_lane = lane < hd
        lane1 = lax.broadcasted_iota(jnp.int32, (1, LANES), 1)
        ones = jnp.ones((ONES_ROWS, blk), BF16)
        for n in range(nb):
            rows = slice(n * blk, (n + 1) * blk)
            kb = k_ref[0, rows, :]
            mean = jnp.sum(kb.astype(F32), axis=0, keepdims=True) * (1.0 / blk)
            kmean_ref[n:n + 1, :] = jnp.where(lane1 < hd, mean, 0.0)
            kmean_ref[nb + n:nb + n + 1, :] = jnp.where(lane1 < hd, 0.0, mean)
            kaug0_ref[n] = jnp.where(head0_lane, kb, (lane == hd + n).astype(BF16))
            kaug1_ref[n] = jnp.where(head0_lane, (lane == n).astype(BF16), kb)
            vt = v_ref[0, rows, :].astype(F32).T.astype(BF16)
            vt0_ref[n] = jnp.concatenate([vt[:hd], ones], axis=0)
            vt1_ref[n] = jnp.concatenate([vt[hd:], ones], axis=0)

    qt = (q_ref[0].astype(F32) * (hd ** -0.5 * LOG2_E)).T.astype(BF16)
    km = kmean_ref[...]
    km_hi = km.astype(BF16)
    km_lo = (km - km_hi.astype(F32)).astype(BF16)
    gate = _dot(km_hi, qt) + _dot(km_lo, qt)
    first = 2 * t
    zeros = lambda r: jnp.zeros((r, Q_TILE), BF16)

    kpos = lax.broadcasted_iota(jnp.int32, (blk, blk), 0)
    qpos = lax.broadcasted_iota(jnp.int32, (blk, blk), 1)
    causal = kpos <= qpos
    lo, hi = slice(0, blk), slice(blk, Q_TILE)

    def partial(s, vts):
        m_blk = jnp.max(s, axis=0, keepdims=True)
        p = jnp.exp2(s - m_blk).astype(BF16)
        w = s.shape[1] // len(vts)
        parts = [_dot(vt, p[:, j * w:(j + 1) * w]) for j, vt in enumerate(vts)]
        return m_blk, parts[0] if len(parts) == 1 else jnp.concatenate(parts, axis=1)

    def merge(m, acc, m_blk, part):
        m_new = jnp.maximum(m, m_blk)
        return m_new, acc * jnp.exp2(m - m_new) + part * jnp.exp2(m_blk - m_new)

    qt_own = (jnp.concatenate([qt[:hd], zeros(LANES - hd)], axis=0),
              jnp.concatenate([zeros(hd), qt[hd:]], axis=0))
    for head, kaug_ref in enumerate((kaug0_ref, kaug1_ref)):
        sa_ref[head, :, lo] = jnp.where(causal, _dot(kaug_ref[first], qt_own[head][:, lo]), MASKED)
        sa_ref[head, :, hi] = jnp.where(causal, _dot(kaug_ref[first + 1], qt_own[head][:, hi]), MASKED)

    q_block = first + (lax.broadcasted_iota(jnp.int32, (1, Q_TILE), 1) >= blk).astype(jnp.int32)
    mask0 = _moba_select(gate[:nb], q_block).astype(BF16)
    mask1 = _moba_select(gate[nb:], q_block).astype(BF16)
    qt0 = jnp.concatenate([qt[:hd], mask0, zeros(LANES - hd - nb)], axis=0)
    qt1 = jnp.concatenate([mask1, zeros(hd - nb), qt[hd:]], axis=0)

    def issue_logits(s_ref, n):
        s_ref[0] = _dot(kaug0_ref[n], qt0)
        s_ref[1] = _dot(kaug1_ref[n], qt1)

    def consume(s_ref, n, carry):
        out = ()
        for head, vt_ref in enumerate((vt0_ref, vt1_ref)):
            out += merge(carry[2 * head], carry[2 * head + 1], *partial(s_ref[head], [vt_ref[n]]))
        return out

    sb_ref[0, :, lo] = _dot(kaug0_ref[first], qt0[:, hi])
    sb_ref[1, :, lo] = _dot(kaug1_ref[first], qt1[:, hi])
    diag = [partial(sa_ref[head], [vt_ref[first], vt_ref[first + 1]])
            for head, vt_ref in enumerate((vt0_ref, vt1_ref))]
    issue_logits(sa_ref, 0)
    carry = ()
    for head, vt_ref in enumerate((vt0_ref, vt1_ref)):
        m, acc = diag[head]
        m_hi, acc_hi = merge(m[:, hi], acc[:, hi], *partial(sb_ref[head, :, lo], [vt_ref[first]]))
        carry += (jnp.concatenate([m[:, lo], m_hi], axis=1),
                  jnp.concatenate([acc[:, lo], acc_hi], axis=1))

    def two_blocks(n, carry):
        issue_logits(sb_ref, n + 1)
        carry = consume(sa_ref, n, carry)
        issue_logits(sa_ref, n + 2)
        return consume(sb_ref, n + 1, carry)

    carry = lax.fori_loop(0, t // 2, lambda j, c: two_blocks(4 * j + 2, two_blocks(4 * j, c)), carry)
    _, a0, _, a1 = lax.fori_loop(0, t % 2, lambda _, c: two_blocks(first - 2, c), carry)
    out_t = jnp.concatenate([a0[:hd] / a0[hd:hd + 1], a1[:hd] / a1[hd:hd + 1]], axis=0)
    o_ref[0] = out_t.T.astype(BF16)


def _moba(u):
    b, s, _ = u.shape
    nb = s // MOBA_BLOCK
    assert nb % ONES_ROWS == 0 and nb <= ATTN_HEAD_DIM and s % Q_TILE == 0
    pairs = ATTN_WIDTH // LANES
    return pl.pallas_call(
        functools.partial(_moba_kernel, nb=nb),
        grid=(b, pairs, s // Q_TILE),
        in_specs=[
            pl.BlockSpec((1, Q_TILE, LANES), lambda bi, hp, t: (bi, t, hp)),
            pl.BlockSpec((1, s, LANES), lambda bi, hp, t: (bi, 0, pairs + hp)),
            pl.BlockSpec((1, s, LANES), lambda bi, hp, t: (bi, 0, 2 * pairs + hp)),
        ],
        out_specs=pl.BlockSpec((1, Q_TILE, LANES), lambda bi, hp, t: (bi, t, hp)),
        out_shape=jax.ShapeDtypeStruct((b, s, ATTN_WIDTH), BF16),
        scratch_shapes=[
            pltpu.VMEM((nb, MOBA_BLOCK, LANES), BF16),
            pltpu.VMEM((nb, MOBA_BLOCK, LANES), BF16),
            pltpu.VMEM((nb, VT_ROWS, MOBA_BLOCK), BF16),
            pltpu.VMEM((nb, VT_ROWS, MOBA_BLOCK), BF16),
            pltpu.VMEM((2 * nb, LANES), F32),
            pltpu.VMEM((2, MOBA_BLOCK, Q_TILE), F32),
            pltpu.VMEM((2, MOBA_BLOCK, Q_TILE), F32),
        ],
        compiler_params=_params("parallel", "parallel", "arbitrary"),
        name="moba",
    )(u, u, u)


def _ret_kernel(q_ref, k_ref, v_ref, g_ref, cos_ref, sin_ref, dmask_ref, kdec_ref, qdec_ref,
                gch_ref, gn_ref, o_ref, state_ref):
    @pl.when(pl.program_id(1) == 0)
    def _():
        state_ref[...] = jnp.zeros_like(state_ref)

    cos = cos_ref[...]
    sin = sin_ref[...]
    lane = lax.broadcasted_iota(jnp.int32, cos.shape, 1)
    even = (lane & 1) == 0

    def rot(x):
        partner = jnp.where(even, pltpu.roll(x, LANES - 1, 1), pltpu.roll(x, 1, 1))
        return x * cos + partner * sin

    heads = range(RET_HEADS)
    cols = [slice(h * RET_HEAD_DIM, (h + 1) * RET_HEAD_DIM) for h in heads]
    q = [rot(q_ref[0, :, c].astype(F32)) for c in cols]
    k = [rot(k_ref[0, :, c].astype(F32)) * (RET_HEAD_DIM ** -0.5) for c in cols]
    v = [v_ref[0, :, c] for c in cols]
    state = [state_ref[h] for h in heads]
    scores = [_dot_nt(q[h].astype(BF16), k[h].astype(BF16)) for h in heads]
    cross = [_dot((q[h] * qdec_ref[h]).astype(BF16), state[h].astype(BF16)) for h in heads]
    k_dec = [(k[h] * kdec_ref[h]).T.astype(BF16) for h in heads]
    for h in heads:
        state_ref[h] = state[h] * gch_ref[h, 0:1, :] + _dot(k_dec[h], v[h])
    y = [_dot((scores[h] * dmask_ref[h]).astype(BF16), v[h]) + cross[h] for h in heads]
    for h in heads:
        gate = g_ref[0, :, cols[h]].astype(F32)
        o_ref[0, :, cols[h]] = (gate * _sigmoid(gate) * _rms(y[h], gn_ref[h])).astype(BF16)


def _retention(u, ret_norm_g):
    b, s, _ = u.shape
    c = RET_CHUNK
    nc = s // c
    h = RET_HEADS
    base = 3 * ATTN_WIDTH // RET_WIDTH

    inv = 1.0 / (ROPE_BASE ** jnp.linspace(0.0, 1.0, RET_HEAD_DIM // 2, dtype=F32))
    ang = jnp.arange(s)[:, None].astype(F32) * inv[None, :]
    cos_t = jnp.repeat(jnp.cos(ang), 2, axis=-1)
    sin_t = jnp.repeat(jnp.sin(ang), 2, axis=-1) * jnp.tile(jnp.array([-1.0, 1.0], F32), RET_HEAD_DIM // 2)
    log_g = jnp.log1p(-jnp.exp2(-5.0 - jnp.arange(h, dtype=F32)))
    i = jnp.arange(c, dtype=F32)
    rel = i[:, None] - i[None, :]
    dmask = jnp.where(rel[None] >= 0, jnp.exp(jnp.maximum(rel, 0.0)[None] * log_g[:, None, None]), 0.0)
    k_dec = jnp.exp((c - 1 - i)[None, :] * log_g[:, None])
    q_dec = jnp.exp((i + 1.0)[None, :] * log_g[:, None])
    g_chunk = jnp.exp(c * log_g)
    k_dec = jnp.broadcast_to(k_dec[:, :, None], (h, c, LANES))
    q_dec = jnp.broadcast_to(q_dec[:, :, None], (h, c, LANES))
    g_chunk = jnp.broadcast_to(g_chunk[:, None, None], (h, 8, LANES))
    gn = ret_norm_g.reshape(h, 1, RET_HEAD_DIM)

    def col(off):
        return pl.BlockSpec((1, c, RET_WIDTH), lambda bi, n: (bi, n, base + off))

    return pl.pallas_call(
        _ret_kernel,
        grid=(b, nc),
        in_specs=[
            col(0), col(1), col(2), col(3),
            pl.BlockSpec((c, LANES), lambda bi, n: (n, 0)),
            pl.BlockSpec((c, LANES), lambda bi, n: (n, 0)),
            _resident((h, c, c)), _resident((h, c, LANES)), _resident((h, c, LANES)),
            _resident((h, 8, LANES)), _resident((h, 1, RET_HEAD_DIM)),
        ],
        out_specs=pl.BlockSpec((1, c, RET_WIDTH), lambda bi, n: (bi, n, 0)),
        out_shape=jax.ShapeDtypeStruct((b, s, RET_WIDTH), BF16),
        scratch_shapes=[pltpu.VMEM((h, RET_HEAD_DIM, RET_HEAD_DIM), F32)],
        compiler_params=_params("parallel", "arbitrary"),
        name="retention",
    )(u, u, u, u, cos_t, sin_t, dmask, k_dec, q_dec, g_chunk, gn)


def _tail_kernel(h_ref, a_ref, r_ref, p_ref, wa_ref, wr_ref, g_ffn_ref, w_in_ref, w_out_ref,
                 g_ple_ref, wpg_ref, wpp_ref, g_final_ref, o_ref, act_ref, *, final_norm):
    h1 = h_ref[...] + _dot(a_ref[...], wa_ref[...]) + _dot(r_ref[...], wr_ref[...])
    n1 = _rms(h1, g_ffn_ref[...]).astype(BF16)
    proj = _dot(p_ref[...].astype(BF16), wpp_ref[...])
    for c in range(N_FF_CHUNKS):
        zg = _dot(n1, w_in_ref[:, c * FF_CHUNK:(c + 1) * FF_CHUNK])
        zu = _dot(n1, w_in_ref[:, D_FF + c * FF_CHUNK:D_FF + (c + 1) * FF_CHUNK])
        act_ref[:, c * FF_CHUNK:(c + 1) * FF_CHUNK] = (zg * _sigmoid(zg) * zu).astype(BF16)
    h2 = h1 + _dot(act_ref[...], w_out_ref[...])
    gate = _sigmoid(_dot(_rms(h2, g_ple_ref[...]).astype(BF16), wpg_ref[...]))
    out = h2 + gate * proj
    if final_norm:
        out = _rms(out, g_final_ref[...])
    o_ref[...] = out


def _tail(h, a, r, p, wa, wr, g_ffn, w_ffn_in, w_ffn_out, g_ple, wpg, wpp, g_final, final_norm):
    t = h.shape[0]
    row = lambda w: pl.BlockSpec((ROW_TILE, w), lambda i: (i, 0))
    return pl.pallas_call(
        functools.partial(_tail_kernel, final_norm=final_norm),
        grid=(t // ROW_TILE,),
        in_specs=[row(D_MODEL), row(ATTN_WIDTH), row(RET_WIDTH), row(PLE_DIM),
                  _resident((ATTN_WIDTH, D_MODEL)), _resident((RET_WIDTH, D_MODEL)),
                  _resident((1, D_MODEL)), _resident((D_MODEL, 2 * D_FF)),
                  _resident((D_FF, D_MODEL)), _resident((1, D_MODEL)),
                  _resident((D_MODEL, D_MODEL)), _resident((PLE_DIM, D_MODEL)),
                  _resident((1, D_MODEL))],
        out_specs=row(D_MODEL),
        out_shape=jax.ShapeDtypeStruct((t, D_MODEL), F32),
        scratch_shapes=[pltpu.VMEM((ROW_TILE, D_FF), BF16)],
        compiler_params=_params("parallel"),
        name="tail",
    )(h, a, r, p, wa, wr, g_ffn, w_ffn_in, w_ffn_out, g_ple, wpg, wpp, g_final)


def kernel(x, p, attn_norm_g, w_in, ret_norm_g, w_out, ffn_norm_g, w_ffn_in, w_ffn_out,
           ple_norm_g, w_ple_gate, w_ple_proj, final_norm_g):
    b, s, d = x.shape
    depth = p.shape[0]
    assert d == D_MODEL and s % MOBA_BLOCK == 0 and s >= (MOBA_TOPK + 1) * MOBA_BLOCK
    assert (b * s) % ROW_TILE == 0 and s // MOBA_BLOCK <= LANES - ATTN_HEAD_DIM
    t = b * s
    h = x.reshape(t, d)
    gain = lambda g: g.reshape(1, -1).astype(F32)
    for i in range(depth):
        u = _in_proj(h, gain(attn_norm_g[i]), w_in[i].astype(BF16)).reshape(b, s, IN_WIDTH)
        a = _moba(u).reshape(t, ATTN_WIDTH)
        r = _retention(u, ret_norm_g[i].astype(F32)).reshape(t, RET_WIDTH)
        wo = w_out[i].astype(BF16)
        h = _tail(h, a, r, p[i].reshape(t, PLE_DIM), wo[:ATTN_WIDTH], wo[ATTN_WIDTH:],
                  gain(ffn_norm_g[i]), w_ffn_in[i].astype(BF16), w_ffn_out[i].astype(BF16),
                  gain(ple_norm_g[i]), w_ple_gate[i].astype(BF16), w_ple_proj[i].astype(BF16),
                  gain(final_norm_g), final_norm=(i == depth - 1))
    return h.reshape(b, s, d)
```

```python
import functools

import jax
import jax.numpy as jnp
from jax import lax
from jax.experimental import pallas as pl
from jax.experimental.pallas import tpu as pltpu

D_MODEL = 1024
PLE_DIM = 256
ATTN_WIDTH = 512
ATTN_HEAD_DIM = 64
RET_WIDTH = 512
RET_HEADS = 4
RET_HEAD_DIM = 128
IN_WIDTH = 3 * ATTN_WIDTH + 4 * RET_WIDTH
MOBA_BLOCK = 256
MOBA_TOPK = 3
RET_CHUNK = 256
ROPE_BASE = 10000.0
D_FF = 2816
EPS = 1e-6

LANES = 128
FF_CHUNK = 256
N_FF_CHUNKS = D_FF // FF_CHUNK
ROW_TILE = 512
IN_COL_CHUNK = 512
MASKED = -1e30
LOG2_E = 1.4426950408889634
VMEM_LIMIT = 52 * 1024 * 1024

F32 = jnp.float32
BF16 = jnp.bfloat16


def _dot(a, b):
    return jnp.dot(a, b, preferred_element_type=F32)


def _dot_nt(a, b):
    return lax.dot_general(a, b, (((1,), (1,)), ((), ())), preferred_element_type=F32)


def _rms(x, g):
    return x * lax.rsqrt(jnp.mean(x * x, axis=-1, keepdims=True) + EPS) * g


def _sigmoid(x):
    return 1.0 / (1.0 + jnp.exp(-x))


def _resident(shape):
    zeros = (0,) * len(shape)
    return pl.BlockSpec(shape, lambda *_: zeros, pipeline_mode=pl.Buffered(1))


def _params(*sem):
    return pltpu.CompilerParams(dimension_semantics=sem, vmem_limit_bytes=VMEM_LIMIT)


def _in_proj_kernel(x_ref, g_ref, w_ref, o_ref):
    n = _rms(x_ref[...], g_ref[...]).astype(BF16)
    for c in range(IN_WIDTH // IN_COL_CHUNK):
        cols = slice(c * IN_COL_CHUNK, (c + 1) * IN_COL_CHUNK)
        o_ref[:, cols] = _dot(n, w_ref[:, cols]).astype(BF16)


def _in_proj(h, g, w):
    t = h.shape[0]
    return pl.pallas_call(
        _in_proj_kernel,
        grid=(t // ROW_TILE,),
        in_specs=[
            pl.BlockSpec((ROW_TILE, D_MODEL), lambda i: (i, 0)),
            _resident((1, D_MODEL)),
            _resident((D_MODEL, IN_WIDTH)),
        ],
        out_specs=pl.BlockSpec((ROW_TILE, IN_WIDTH), lambda i: (i, 0)),
        out_shape=jax.ShapeDtypeStruct((t, IN_WIDTH), BF16),
        compiler_params=_params("parallel"),
        name="in_proj",
    )(h, g, w)


ONES_ROWS = 16
VT_ROWS = ATTN_HEAD_DIM + ONES_ROWS
Q_BLOCKS = 4
Q_TILE = Q_BLOCKS * MOBA_BLOCK

def _moba_select(gate, q_block):
    nb = gate.shape[0]
    row = lax.broadcasted_iota(jnp.int32, gate.shape, 0)
    past = row < q_block
    g = jnp.where(past, gate, -jnp.inf)
    chosen = row == q_block
    for _ in range(MOBA_TOPK):
        top = jnp.max(g, axis=0, keepdims=True)
        hit = row == jnp.min(jnp.where(g == top, row, nb), axis=0, keepdims=True)
        chosen = chosen | (hit & past)
        g = jnp.where(hit, -jnp.inf, g)
    return jnp.where(chosen, 0.0, MASKED)


def _moba_kernel(q_ref, k_ref, v_ref, o_ref, kaug0_ref, kaug1_ref, vt0_ref, vt1_ref, kmean_ref,
                 sa_ref, sb_ref, *, nb):
    t = pl.program_id(2)
    blk = MOBA_BLOCK
    hd = ATTN_HEAD_DIM

    @pl.when(t == 0)
    def _():
        lane = lax.broadcasted_iota(jnp.int32, (blk, LANES), 1)
        head0_lane = lane < hd
        lane1 = lax.broadcasted_iota(jnp.int32, (1, LANES), 1)
        ones = jnp.ones((ONES_ROWS, blk), BF16)
        for n in range(nb):
            rows = slice(n * blk, (n + 1) * blk)
            kb = k_ref[0, rows, :]
            mean = jnp.sum(kb.astype(F32), axis=0, keepdims=True) * (1.0 / blk)
            kmean_ref[n:n + 1, :] = jnp.where(lane1 < hd, mean, 0.0)
            kmean_ref[nb + n:nb + n + 1, :] = jnp.where(lane1 < hd, 0.0, mean)
            kaug0_ref[n] = jnp.where(head0_lane, kb, (lane == hd + n).astype(BF16))
            kaug1_ref[n] = jnp.where(head0_lane, (lane == n).astype(BF16), kb)
            vt = v_ref[0, rows, :].astype(F32).T.astype(BF16)
            vt0_ref[n] = jnp.concatenate([vt[:hd], ones], axis=0)
            vt1_ref[n] = jnp.concatenate([vt[hd:], ones], axis=0)

    qt = (q_ref[0].astype(F32) * (hd ** -0.5 * LOG2_E)).T.astype(BF16)
    km = kmean_ref[...]
    km_hi = km.astype(BF16)
    km_lo = (km - km_hi.astype(F32)).astype(BF16)
    gate = _dot(km_hi, qt) + _dot(km_lo, qt)
    first = Q_BLOCKS * t
    zeros = lambda r: jnp.zeros((r, Q_TILE), BF16)
    kaug_refs = (kaug0_ref, kaug1_ref)
    vt_refs = (vt0_ref, vt1_ref)

    kpos = lax.broadcasted_iota(jnp.int32, (blk, blk), 0)
    qpos = lax.broadcasted_iota(jnp.int32, (blk, blk), 1)
    causal = kpos <= qpos
    group = lambda j: slice(j * blk, (j + 1) * blk)

    def softmax_pv(s, m_ref, vts):
        p = jnp.exp2(s - m_ref).astype(BF16)
        w = s.shape[1] // len(vts)
        parts = [_dot(vt, p[:, j * w:(j + 1) * w]) for j, vt in enumerate(vts)]
        return parts[0] if len(parts) == 1 else jnp.concatenate(parts, axis=1)

    qt_own = (jnp.concatenate([qt[:hd], zeros(LANES - hd)], axis=0),
              jnp.concatenate([zeros(hd), qt[hd:]], axis=0))
    for head in range(2):
        for j in range(Q_BLOCKS):
            logits = _dot(kaug_refs[head][first + j], qt_own[head][:, group(j)])
            sa_ref[head, :, group(j)] = jnp.where(causal, logits, MASKED)

    lane = lax.broadcasted_iota(jnp.int32, (1, Q_TILE), 1)
    q_block = first + sum((lane >= j * blk).astype(jnp.int32) for j in range(1, Q_BLOCKS))
    mask0 = _moba_select(gate[:nb], q_block).astype(BF16)
    mask1 = _moba_select(gate[nb:], q_block).astype(BF16)
    qt_sel = (jnp.concatenate([qt[:hd], mask0, zeros(LANES - hd - nb)], axis=0),
              jnp.concatenate([mask1, zeros(hd - nb), qt[hd:]], axis=0))

    def issue_logits(s_ref, n, skip=0):
        for head in range(2):
            logits = _dot(kaug_refs[head][n], qt_sel[head][:, skip * blk:])
            s_ref[head, :, :Q_TILE - skip * blk] = logits

    def consume(s_ref, n, carry, skip=0):
        out = ()
        c = skip * blk
        for head in range(2):
            m, acc = carry[2 * head], carry[2 * head + 1]
            s = s_ref[head, :, :Q_TILE - c]
            m_new = jnp.maximum(m[:, c:], jnp.max(s, axis=0, keepdims=True))
            acc_new = acc[:, c:] * jnp.exp2(m[:, c:] - m_new) + softmax_pv(s, m_new, [vt_refs[head][n]])
            if skip:
                m_new = jnp.concatenate([m[:, :c], m_new], axis=1)
                acc_new = jnp.concatenate([acc[:, :c], acc_new], axis=1)
            out += (m_new, acc_new)
        return out

    bufs = (sb_ref, sa_ref)
    issue_logits(sb_ref, first, skip=1)
    carry = ()
    for head in range(2):
        s = sa_ref[head]
        m = jnp.max(s, axis=0, keepdims=True)
        carry += (m, softmax_pv(s, m, [vt_refs[head][first + j] for j in range(Q_BLOCKS)]))
    for j in range(Q_BLOCKS - 1):
        if j + 1 < Q_BLOCKS - 1:
            issue_logits(bufs[(j + 1) % 2], first + j + 1, skip=j + 2)
        else:
            issue_logits(bufs[(j + 1) % 2], 0)
        carry = consume(bufs[j % 2], first + j, carry, skip=j + 1)

    def two_blocks(n, carry):
        issue_logits(sb_ref, n + 1)
        carry = consume(sa_ref, n, carry)
        issue_logits(sa_ref, n + 2)
        return consume(sb_ref, n + 1, carry)

    carry = lax.fori_loop(
        0, first // 4, lambda j, c: two_blocks(4 * j + 2, two_blocks(4 * j, c)), carry)
    if Q_BLOCKS % 4:
        carry = lax.fori_loop(0, (first % 4) // 2, lambda _, c: two_blocks(first - 2, c), carry)
    _, a0, _, a1 = carry
    out_t = jnp.concatenate([a0[:hd] / a0[hd:hd + 1], a1[:hd] / a1[hd:hd + 1]], axis=0)
    o_ref[0] = out_t.T.astype(BF16)


def _moba(u):
    b, s, _ = u.shape
    nb = s // MOBA_BLOCK
    assert nb % ONES_ROWS == 0 and nb <= ATTN_HEAD_DIM and s % Q_TILE == 0 and Q_BLOCKS % 2 == 0
    pairs = ATTN_WIDTH // LANES
    return pl.pallas_call(
        functools.partial(_moba_kernel, nb=nb),
        grid=(b, pairs, s // Q_TILE),
        in_specs=[
            pl.BlockSpec((1, Q_TILE, LANES), lambda bi, hp, t: (bi, t, hp)),
            pl.BlockSpec((1, s, LANES), lambda bi, hp, t: (bi, 0, pairs + hp)),
            pl.BlockSpec((1, s, LANES), lambda bi, hp, t: (bi, 0, 2 * pairs + hp)),
        ],
        out_specs=pl.BlockSpec((1, Q_TILE, LANES), lambda bi, hp, t: (bi, t, hp)),
        out_shape=jax.ShapeDtypeStruct((b, s, ATTN_WIDTH), BF16),
        scratch_shapes=[
            pltpu.VMEM((nb, MOBA_BLOCK, LANES), BF16),
            pltpu.VMEM((nb, MOBA_BLOCK, LANES), BF16),
            pltpu.VMEM((nb, VT_ROWS, MOBA_BLOCK), BF16),
            pltpu.VMEM((nb, VT_ROWS, MOBA_BLOCK), BF16),
            pltpu.VMEM((2 * nb, LANES), F32),
            pltpu.VMEM((2, MOBA_BLOCK, Q_TILE), F32),
            pltpu.VMEM((2, MOBA_BLOCK, Q_TILE), F32),
        ],
        compiler_params=_params("parallel", "parallel", "arbitrary"),
        name="moba",
    )(u, u, u)


def _ret_kernel(q_ref, k_ref, v_ref, g_ref, cos_ref, sin_ref, dmask_ref, kdec_ref, qdec_ref,
                gch_ref, gn_ref, o_ref, state_ref):
    @pl.when(pl.program_id(1) == 0)
    def _():
        state_ref[...] = jnp.zeros_like(state_ref)

    cos = cos_ref[...]
    sin = sin_ref[...]
    lane = lax.broadcasted_iota(jnp.int32, cos.shape, 1)
    even = (lane & 1) == 0

    def rot(x):
        partner = jnp.where(even, pltpu.roll(x, LANES - 1, 1), pltpu.roll(x, 1, 1))
        return x * cos + partner * sin

    heads = range(RET_HEADS)
    cols = [slice(h * RET_HEAD_DIM, (h + 1) * RET_HEAD_DIM) for h in heads]
    q = [rot(q_ref[0, :, c].astype(F32)) for c in cols]
    k = [rot(k_ref[0, :, c].astype(F32)) * (RET_HEAD_DIM ** -0.5) for c in cols]
    v = [v_ref[0, :, c] for c in cols]
    state = [state_ref[h] for h in heads]
    scores = [_dot_nt(q[h].astype(BF16), k[h].astype(BF16)) for h in heads]
    cross = [_dot((q[h] * qdec_ref[h]).astype(BF16), state[h].astype(BF16)) for h in heads]
    k_dec = [(k[h] * kdec_ref[h]).T.astype(BF16) for h in heads]
    for h in heads:
        state_ref[h] = state[h] * gch_ref[h, 0:1, :] + _dot(k_dec[h], v[h])
    y = [_dot((scores[h] * dmask_ref[h]).astype(BF16), v[h]) + cross[h] for h in heads]
    for h in heads:
        gate = g_ref[0, :, cols[h]].astype(F32)
        o_ref[0, :, cols[h]] = (gate * _sigmoid(gate) * _rms(y[h], gn_ref[h])).astype(BF16)


def _retention(u, ret_norm_g):
    b, s, _ = u.shape
    c = RET_CHUNK
    nc = s // c
    h = RET_HEADS
    base = 3 * ATTN_WIDTH // RET_WIDTH

    inv = 1.0 / (ROPE_BASE ** jnp.linspace(0.0, 1.0, RET_HEAD_DIM // 2, dtype=F32))
    ang = jnp.arange(s)[:, None].astype(F32) * inv[None, :]
    cos_t = jnp.repeat(jnp.cos(ang), 2, axis=-1)
    sin_t = jnp.repeat(jnp.sin(ang), 2, axis=-1) * jnp.tile(jnp.array([-1.0, 1.0], F32), RET_HEAD_DIM // 2)
    log_g = jnp.log1p(-jnp.exp2(-5.0 - jnp.arange(h, dtype=F32)))
    i = jnp.arange(c, dtype=F32)
    rel = i[:, None] - i[None, :]
    dmask = jnp.where(rel[None] >= 0, jnp.exp(jnp.maximum(rel, 0.0)[None] * log_g[:, None, None]), 0.0)
    k_dec = jnp.exp((c - 1 - i)[None, :] * log_g[:, None])
    q_dec = jnp.exp((i + 1.0)[None, :] * log_g[:, None])
    g_chunk = jnp.exp(c * log_g)
    k_dec = jnp.broadcast_to(k_dec[:, :, None], (h, c, LANES))
    q_dec = jnp.broadcast_to(q_dec[:, :, None], (h, c, LANES))
    g_chunk = jnp.broadcast_to(g_chunk[:, None, None], (h, 8, LANES))
    gn = ret_norm_g.reshape(h, 1, RET_HEAD_DIM)

    def col(off):
        return pl.BlockSpec((1, c, RET_WIDTH), lambda bi, n: (bi, n, base + off))

    return pl.pallas_call(
        _ret_kernel,
        grid=(b, nc),
        in_specs=[
            col(0), col(1), col(2), col(3),
            pl.BlockSpec((c, LANES), lambda bi, n: (n, 0)),
            pl.BlockSpec((c, LANES), lambda bi, n: (n, 0)),
            _resident((h, c, c)), _resident((h, c, LANES)), _resident((h, c, LANES)),
            _resident((h, 8, LANES)), _resident((h, 1, RET_HEAD_DIM)),
        ],
        out_specs=pl.BlockSpec((1, c, RET_WIDTH), lambda bi, n: (bi, n, 0)),
        out_shape=jax.ShapeDtypeStruct((b, s, RET_WIDTH), BF16),
        scratch_shapes=[pltpu.VMEM((h, RET_HEAD_DIM, RET_HEAD_DIM), F32)],
        compiler_params=_params("parallel", "arbitrary"),
        name="retention",
    )(u, u, u, u, cos_t, sin_t, dmask, k_dec, q_dec, g_chunk, gn)


def _tail_kernel(h_ref, a_ref, r_ref, p_ref, wo_ref, g_ffn_ref, w_in_ref, w_out_ref,
                 g_ple_ref, wpg_ref, wpp_ref, g_final_ref, o_ref, act_ref, *, final_norm):
    h1 = (h_ref[...] + _dot(a_ref[...], wo_ref[:ATTN_WIDTH, :])
          + _dot(r_ref[...], wo_ref[ATTN_WIDTH:, :]))
    n1 = _rms(h1, g_ffn_ref[...]).astype(BF16)
    proj = _dot(p_ref[...].astype(BF16), wpp_ref[...])
    for c in range(N_FF_CHUNKS):
        zg = _dot(n1, w_in_ref[:, c * FF_CHUNK:(c + 1) * FF_CHUNK])
        zu = _dot(n1, w_in_ref[:, D_FF + c * FF_CHUNK:D_FF + (c + 1) * FF_CHUNK])
        act_ref[:, c * FF_CHUNK:(c + 1) * FF_CHUNK] = (zg * _sigmoid(zg) * zu).astype(BF16)
    h2 = h1 + _dot(act_ref[...], w_out_ref[...])
    gate = _sigmoid(_dot(_rms(h2, g_ple_ref[...]).astype(BF16), wpg_ref[...]))
    out = h2 + gate * proj
    if final_norm:
        out = _rms(out, g_final_ref[...])
    o_ref[...] = out


def _tail(h, a, r, p, layer, wo, g_ffn, w_ffn_in, w_ffn_out, g_ple, wpg, wpp, g_final, final_norm):
    t = h.shape[0]
    steps = t // ROW_TILE
    row = lambda w: pl.BlockSpec((ROW_TILE, w), lambda i: (i, 0))
    return pl.pallas_call(
        functools.partial(_tail_kernel, final_norm=final_norm),
        grid=(steps,),
        in_specs=[row(D_MODEL), row(ATTN_WIDTH), row(RET_WIDTH),
                  pl.BlockSpec((ROW_TILE, PLE_DIM), lambda i: (layer * steps + i, 0)),
                  _resident((ATTN_WIDTH + RET_WIDTH, D_MODEL)),
                  _resident((1, D_MODEL)), _resident((D_MODEL, 2 * D_FF)),
                  _resident((D_FF, D_MODEL)), _resident((1, D_MODEL)),
                  _resident((D_MODEL, D_MODEL)), _resident((PLE_DIM, D_MODEL)),
                  _resident((1, D_MODEL))],
        out_specs=row(D_MODEL),
        out_shape=jax.ShapeDtypeStruct((t, D_MODEL), F32),
        scratch_shapes=[pltpu.VMEM((ROW_TILE, D_FF), BF16)],
        compiler_params=_params("parallel"),
        name="tail",
    )(h, a, r, p, wo, g_ffn, w_ffn_in, w_ffn_out, g_ple, wpg, wpp, g_final)


def kernel(x, p, attn_norm_g, w_in, ret_norm_g, w_out, ffn_norm_g, w_ffn_in, w_ffn_out,
           ple_norm_g, w_ple_gate, w_ple_proj, final_norm_g):
    b, s, d = x.shape
    depth = p.shape[0]
    assert d == D_MODEL and s % MOBA_BLOCK == 0 and s >= (MOBA_TOPK + 1) * MOBA_BLOCK
    assert (b * s) % ROW_TILE == 0 and s // MOBA_BLOCK <= LANES - ATTN_HEAD_DIM
    t = b * s
    h = x.reshape(t, d)
    gain = lambda g: g.reshape(1, -1).astype(F32)
    p_rows = p.reshape(depth * t, PLE_DIM)
    for i in range(depth):
        u = _in_proj(h, gain(attn_norm_g[i]), w_in[i].astype(BF16)).reshape(b, s, IN_WIDTH)
        a = _moba(u).reshape(t, ATTN_WIDTH)
        r = _retention(u, ret_norm_g[i].astype(F32)).reshape(t, RET_WIDTH)
        h = _tail(h, a, r, p_rows, i, w_out[i].astype(BF16),
                  gain(ffn_norm_g[i]), w_ffn_in[i].astype(BF16), w_ffn_out[i].astype(BF16),
                  gain(ple_norm_g[i]), w_ple_gate[i].astype(BF16), w_ple_proj[i].astype(BF16),
                  gain(final_norm_g), final_norm=(i == depth - 1))
    return h.reshape(b, s, d)
```

```python
import functools

import jax
import jax.numpy as jnp
from jax import lax
from jax.experimental import pallas as pl
from jax.experimental.pallas import tpu as pltpu

D_MODEL = 1024
PLE_DIM = 256
ATTN_WIDTH = 512
ATTN_HEAD_DIM = 64
RET_WIDTH = 512
RET_HEADS = 4
RET_HEAD_DIM = 128
IN_WIDTH = 3 * ATTN_WIDTH + 4 * RET_WIDTH
MOBA_BLOCK = 256
MOBA_TOPK = 3
RET_CHUNK = 256
ROPE_BASE = 10000.0
D_FF = 2816
EPS = 1e-6

LANES = 128
FF_CHUNK = 256
N_FF_CHUNKS = D_FF // FF_CHUNK
ROW_TILE = 512
RET_SEQS = 2
IN_COL_CHUNK = 512
MASKED = -1e30
LOG2_E = 1.4426950408889634
VMEM_LIMIT = 52 * 1024 * 1024

F32 = jnp.float32
BF16 = jnp.bfloat16


def _dot(a, b):
    return jnp.dot(a, b, preferred_element_type=F32)


def _dot_nt(a, b):
    return lax.dot_general(a, b, (((1,), (1,)), ((), ())), preferred_element_type=F32)


def _rms(x, g):
    return x * lax.rsqrt(jnp.mean(x * x, axis=-1, keepdims=True) + EPS) * g


def _sigmoid(x):
    return 1.0 / (1.0 + jnp.exp(-x))


def _resident(shape, layer=None):
    zeros = (0,) * len(shape)
    if layer is None:
        return pl.BlockSpec(shape, lambda *_: zeros, pipeline_mode=pl.Buffered(1))
    return pl.BlockSpec((None,) + tuple(shape), lambda *_: (layer,) + zeros,
                        pipeline_mode=pl.Buffered(1))


def _params(*sem):
    return pltpu.CompilerParams(dimension_semantics=sem, vmem_limit_bytes=VMEM_LIMIT)


def _in_proj_kernel(x_ref, g_ref, w_ref, o_ref):
    n = _rms(x_ref[...], g_ref[...]).astype(BF16)
    for c in range(IN_WIDTH // IN_COL_CHUNK):
        cols = slice(c * IN_COL_CHUNK, (c + 1) * IN_COL_CHUNK)
        o_ref[:, cols] = _dot(n, w_ref[:, cols]).astype(BF16)


def _in_proj(h, g, w, layer):
    t = h.shape[0]
    return pl.pallas_call(
        _in_proj_kernel,
        grid=(t // ROW_TILE,),
        in_specs=[
            pl.BlockSpec((ROW_TILE, D_MODEL), lambda i: (i, 0)),
            _resident((1, D_MODEL), layer),
            _resident((D_MODEL, IN_WIDTH), layer),
        ],
        out_specs=pl.BlockSpec((ROW_TILE, IN_WIDTH), lambda i: (i, 0)),
        out_shape=jax.ShapeDtypeStruct((t, IN_WIDTH), BF16),
        compiler_params=_params("parallel"),
        name="in_proj",
    )(h, g, w)


ONES_ROWS = 16
VT_ROWS = ATTN_HEAD_DIM + ONES_ROWS
Q_BLOCKS = 4
Q_TILE = Q_BLOCKS * MOBA_BLOCK
PIPE_COLS = 2 * MOBA_BLOCK

def _moba_select(gate, q_block):
    nb = gate.shape[0]
    row = lax.broadcasted_iota(jnp.int32, gate.shape, 0)
    past = row < q_block
    g = jnp.where(past, gate, -jnp.inf)
    chosen = row == q_block
    for _ in range(MOBA_TOPK):
        top = jnp.max(g, axis=0, keepdims=True)
        hit = row == jnp.min(jnp.where(g == top, row, nb), axis=0, keepdims=True)
        chosen = chosen | (hit & past)
        g = jnp.where(hit, -jnp.inf, g)
    return jnp.where(chosen, 0.0, MASKED)


def _moba_kernel(q_ref, k_ref, v_ref, o_ref, kaug0_ref, kaug1_ref, vt0_ref, vt1_ref, kmean_ref,
                 sa_ref, sb_ref, *, nb):
    t = pl.program_id(2)
    blk = MOBA_BLOCK
    hd = ATTN_HEAD_DIM

    @pl.when(t == 0)
    def _():
        lane = lax.broadcasted_iota(jnp.int32, (blk, LANES), 1)
        head0_lane = lane < hd
        lane1 = lax.broadcasted_iota(jnp.int32, (1, LANES), 1)
        ones = jnp.ones((ONES_ROWS, blk), BF16)
        for n in range(nb):
            rows = slice(n * blk, (n + 1) * blk)
            kb = k_ref[0, rows, :]
            mean = jnp.sum(kb.astype(F32), axis=0, keepdims=True) * (1.0 / blk)
            kmean_ref[n:n + 1, :] = jnp.where(lane1 < hd, mean, 0.0)
            kmean_ref[nb + n:nb + n + 1, :] = jnp.where(lane1 < hd, 0.0, mean)
            kaug0_ref[n] = jnp.where(head0_lane, kb, (lane == hd + n).astype(BF16))
            kaug1_ref[n] = jnp.where(head0_lane, (lane == n).astype(BF16), kb)
            vt = v_ref[0, rows, :].astype(F32).T.astype(BF16)
            vt0_ref[n] = jnp.concatenate([vt[:hd], ones], axis=0)
            vt1_ref[n] = jnp.concatenate([vt[hd:], ones], axis=0)

    qt = (q_ref[0].astype(F32) * (hd ** -0.5 * LOG2_E)).T.astype(BF16)
    km = kmean_ref[...]
    km_hi = km.astype(BF16)
    km_lo = (km - km_hi.astype(F32)).astype(BF16)
    gate = _dot(km_hi, qt) + _dot(km_lo, qt)
    first = Q_BLOCKS * t
    zeros = lambda r: jnp.zeros((r, Q_TILE), BF16)
    kaug_refs = (kaug0_ref, kaug1_ref)
    vt_refs = (vt0_ref, vt1_ref)

    kpos = lax.broadcasted_iota(jnp.int32, (blk, blk), 0)
    qpos = lax.broadcasted_iota(jnp.int32, (blk, blk), 1)
    causal = kpos <= qpos
    group = lambda j: slice(j * blk, (j + 1) * blk)

    def softmax_pv(s, m_ref, vts):
        p = jnp.exp2(s - m_ref).astype(BF16)
        w = s.shape[1] // len(vts)
        parts = [_dot(vt, p[:, j * w:(j + 1) * w]) for j, vt in enumerate(vts)]
        return parts[0] if len(parts) == 1 else jnp.concatenate(parts, axis=1)

    qt_own = (jnp.concatenate([qt[:hd], zeros(LANES - hd)], axis=0),
              jnp.concatenate([zeros(hd), qt[hd:]], axis=0))
    for head in range(2):
        for j in range(Q_BLOCKS):
            logits = _dot(kaug_refs[head][first + j], qt_own[head][:, group(j)])
            sa_ref[head, :, group(j)] = jnp.where(causal, logits, MASKED)

    lane = lax.broadcasted_iota(jnp.int32, (1, Q_TILE), 1)
    q_block = first + sum((lane >= j * blk).astype(jnp.int32) for j in range(1, Q_BLOCKS))
    mask0 = _moba_select(gate[:nb], q_block).astype(BF16)
    mask1 = _moba_select(gate[nb:], q_block).astype(BF16)
    qt_sel = (jnp.concatenate([qt[:hd], mask0, zeros(LANES - hd - nb)], axis=0),
              jnp.concatenate([mask1, zeros(hd - nb), qt[hd:]], axis=0))

    def chunks(skip):
        width = Q_TILE - skip * blk
        return [(a, min(a + PIPE_COLS, width)) for a in range(0, width, PIPE_COLS)]

    def issue_chunk(s_ref, n, head, skip, a, b_):
        q0 = skip * blk
        s_ref[head, :, a:b_] = _dot(kaug_refs[head][n], qt_sel[head][:, q0 + a:q0 + b_])

    def issue_head(s_ref, n, head, skip=0):
        for a, b_ in chunks(skip):
            issue_chunk(s_ref, n, head, skip, a, b_)

    def consume_chunk(s_ref, n, head, m, acc, skip, a, b_):
        q0 = skip * blk
        s = s_ref[head, :, a:b_]
        m_old = m[:, q0 + a:q0 + b_]
        m_new = jnp.maximum(m_old, jnp.max(s, axis=0, keepdims=True))
        return m_new, (acc[:, q0 + a:q0 + b_] * jnp.exp2(m_old - m_new)
                       + softmax_pv(s, m_new, [vt_refs[head][n]]))

    def step(dst_ref, n_issue, src_ref, n_consume, carry, issue_skip=0, consume_skip=0):
        out = ()
        for head in range(2):
            m, acc = carry[2 * head], carry[2 * head + 1]
            q0 = consume_skip * blk
            m_cols = [m[:, :q0]] if q0 else []
            acc_cols = [acc[:, :q0]] if q0 else []
            todo_issue, todo_consume = chunks(issue_skip), chunks(consume_skip)
            for k in range(max(len(todo_issue), len(todo_consume))):
                if k < len(todo_issue):
                    issue_chunk(dst_ref, n_issue, head, issue_skip, *todo_issue[k])
                if k < len(todo_consume):
                    m_new, acc_new = consume_chunk(src_ref, n_consume, head, m, acc, consume_skip,
                                                   *todo_consume[k])
                    m_cols.append(m_new)
                    acc_cols.append(acc_new)
            out += (jnp.concatenate(m_cols, axis=1), jnp.concatenate(acc_cols, axis=1))
        return out

    bufs = (sb_ref, sa_ref)
    for head in range(2):
        issue_head(sb_ref, first, head, skip=1)
    carry = ()
    for head in range(2):
        s = sa_ref[head]
        m = jnp.max(s, axis=0, keepdims=True)
        carry += (m, softmax_pv(s, m, [vt_refs[head][first + j] for j in range(Q_BLOCKS)]))
    for j in range(Q_BLOCKS - 1):
        if j + 1 < Q_BLOCKS - 1:
            carry = step(bufs[(j + 1) % 2], first + j + 1, bufs[j % 2], first + j, carry,
                         issue_skip=j + 2, consume_skip=j + 1)
        else:
            carry = step(bufs[(j + 1) % 2], 0, bufs[j % 2], first + j, carry, consume_skip=j + 1)

    def two_blocks(n, carry):
        carry = step(sb_ref, n + 1, sa_ref, n, carry)
        return step(sa_ref, n + 2, sb_ref, n + 1, carry)

    carry = lax.fori_loop(
        0, first // 4, lambda j, c: two_blocks(4 * j + 2, two_blocks(4 * j, c)), carry)
    if Q_BLOCKS % 4:
        carry = lax.fori_loop(0, (first % 4) // 2, lambda _, c: two_blocks(first - 2, c), carry)
    _, a0, _, a1 = carry
    out_t = jnp.concatenate([a0[:hd] / a0[hd:hd + 1], a1[:hd] / a1[hd:hd + 1]], axis=0)
    o_ref[0] = out_t.T.astype(BF16)


def _moba(u):
    b, s, _ = u.shape
    nb = s // MOBA_BLOCK
    assert nb % ONES_ROWS == 0 and nb <= ATTN_HEAD_DIM and s % Q_TILE == 0 and Q_BLOCKS % 2 == 0
    pairs = ATTN_WIDTH // LANES
    return pl.pallas_call(
        functools.partial(_moba_kernel, nb=nb),
        grid=(b, pairs, s // Q_TILE),
        in_specs=[
            pl.BlockSpec((1, Q_TILE, LANES), lambda bi, hp, t: (bi, t, hp)),
            pl.BlockSpec((1, s, LANES), lambda bi, hp, t: (bi, 0, pairs + hp)),
            pl.BlockSpec((1, s, LANES), lambda bi, hp, t: (bi, 0, 2 * pairs + hp)),
        ],
        out_specs=pl.BlockSpec((1, Q_TILE, LANES), lambda bi, hp, t: (bi, t, hp)),
        out_shape=jax.ShapeDtypeStruct((b, s, ATTN_WIDTH), BF16),
        scratch_shapes=[
            pltpu.VMEM((nb, MOBA_BLOCK, LANES), BF16),
            pltpu.VMEM((nb, MOBA_BLOCK, LANES), BF16),
            pltpu.VMEM((nb, VT_ROWS, MOBA_BLOCK), BF16),
            pltpu.VMEM((nb, VT_ROWS, MOBA_BLOCK), BF16),
            pltpu.VMEM((2 * nb, LANES), F32),
            pltpu.VMEM((2, MOBA_BLOCK, Q_TILE), F32),
            pltpu.VMEM((2, MOBA_BLOCK, Q_TILE), F32),
        ],
        compiler_params=_params("parallel", "parallel", "arbitrary"),
        name="moba",
    )(u, u, u)


def _ret_kernel(q_ref, k_ref, v_ref, g_ref, cos_ref, sin_ref, dmask_ref, kdec_ref, qdec_ref,
                gch_ref, gn_ref, o_ref, state_ref):
    @pl.when(pl.program_id(1) == 0)
    def _():
        state_ref[...] = jnp.zeros_like(state_ref)

    cos = cos_ref[...]
    sin = sin_ref[...]
    lane = lax.broadcasted_iota(jnp.int32, cos.shape, 1)
    even = (lane & 1) == 0

    def rot(x):
        partner = jnp.where(even, pltpu.roll(x, LANES - 1, 1), pltpu.roll(x, 1, 1))
        return x * cos + partner * sin

    jobs = [(i, h) for i in range(RET_SEQS) for h in range(RET_HEADS)]
    col = lambda h: slice(h * RET_HEAD_DIM, (h + 1) * RET_HEAD_DIM)
    q = [rot(q_ref[i, :, col(h)].astype(F32)) for i, h in jobs]
    k = [rot(k_ref[i, :, col(h)].astype(F32)) * (RET_HEAD_DIM ** -0.5) for i, h in jobs]
    v = [v_ref[i, :, col(h)] for i, h in jobs]
    state = [state_ref[i, h] for i, h in jobs]
    scores = [_dot_nt(q[j].astype(BF16), k[j].astype(BF16)) for j in range(len(jobs))]
    cross = [_dot((q[j] * qdec_ref[h]).astype(BF16), state[j].astype(BF16))
             for j, (_, h) in enumerate(jobs)]
    k_dec = [(k[j] * kdec_ref[h]).T.astype(BF16) for j, (_, h) in enumerate(jobs)]
    for j, (i, h) in enumerate(jobs):
        state_ref[i, h] = state[j] * gch_ref[h, 0:1, :] + _dot(k_dec[j], v[j])
    y = [_dot((scores[j] * dmask_ref[h]).astype(BF16), v[j]) + cross[j]
         for j, (_, h) in enumerate(jobs)]
    for j, (i, h) in enumerate(jobs):
        gate = g_ref[i, :, col(h)].astype(F32)
        o_ref[i, :, col(h)] = (gate * _sigmoid(gate) * _rms(y[j], gn_ref[h])).astype(BF16)


def _retention(u, ret_norm_g):
    b, s, _ = u.shape
    c = RET_CHUNK
    nc = s // c
    h = RET_HEADS
    base = 3 * ATTN_WIDTH // RET_WIDTH

    inv = 1.0 / (ROPE_BASE ** jnp.linspace(0.0, 1.0, RET_HEAD_DIM // 2, dtype=F32))
    ang = jnp.arange(s)[:, None].astype(F32) * inv[None, :]
    cos_t = jnp.repeat(jnp.cos(ang), 2, axis=-1)
    sin_t = jnp.repeat(jnp.sin(ang), 2, axis=-1) * jnp.tile(jnp.array([-1.0, 1.0], F32), RET_HEAD_DIM // 2)
    log_g = jnp.log1p(-jnp.exp2(-5.0 - jnp.arange(h, dtype=F32)))
    i = jnp.arange(c, dtype=F32)
    rel = i[:, None] - i[None, :]
    dmask = jnp.where(rel[None] >= 0, jnp.exp(jnp.maximum(rel, 0.0)[None] * log_g[:, None, None]), 0.0)
    k_dec = jnp.exp((c - 1 - i)[None, :] * log_g[:, None])
    q_dec = jnp.exp((i + 1.0)[None, :] * log_g[:, None])
    g_chunk = jnp.exp(c * log_g)
    k_dec = jnp.broadcast_to(k_dec[:, :, None], (h, c, LANES))
    q_dec = jnp.broadcast_to(q_dec[:, :, None], (h, c, LANES))
    g_chunk = jnp.broadcast_to(g_chunk[:, None, None], (h, 8, LANES))
    gn = ret_norm_g.reshape(h, 1, RET_HEAD_DIM)

    assert b % RET_SEQS == 0

    def col(off):
        return pl.BlockSpec((RET_SEQS, c, RET_WIDTH), lambda bi, n: (bi, n, base + off))

    return pl.pallas_call(
        _ret_kernel,
        grid=(b // RET_SEQS, nc),
        in_specs=[
            col(0), col(1), col(2), col(3),
            pl.BlockSpec((c, LANES), lambda bi, n: (n, 0)),
            pl.BlockSpec((c, LANES), lambda bi, n: (n, 0)),
            _resident((h, c, c)), _resident((h, c, LANES)), _resident((h, c, LANES)),
            _resident((h, 8, LANES)), _resident((h, 1, RET_HEAD_DIM)),
        ],
        out_specs=pl.BlockSpec((RET_SEQS, c, RET_WIDTH), lambda bi, n: (bi, n, 0)),
        out_shape=jax.ShapeDtypeStruct((b, s, RET_WIDTH), BF16),
        scratch_shapes=[pltpu.VMEM((RET_SEQS, h, RET_HEAD_DIM, RET_HEAD_DIM), F32)],
        compiler_params=_params("parallel", "arbitrary"),
        name="retention",
    )(u, u, u, u, cos_t, sin_t, dmask, k_dec, q_dec, g_chunk, gn)


def _tail_kernel(h_ref, a_ref, r_ref, p_ref, wo_ref, g_ffn_ref, w_in_ref, w_out_ref,
                 g_ple_ref, wpg_ref, wpp_ref, g_final_ref, o_ref, act_ref, *, final_norm):
    h1 = (h_ref[...] + _dot(a_ref[...], wo_ref[:ATTN_WIDTH, :])
          + _dot(r_ref[...], wo_ref[ATTN_WIDTH:, :]))
    n1 = _rms(h1, g_ffn_ref[...]).astype(BF16)
    proj = _dot(p_ref[...].astype(BF16), wpp_ref[...])
    for c in range(N_FF_CHUNKS):
        zg = _dot(n1, w_in_ref[:, c * FF_CHUNK:(c + 1) * FF_CHUNK])
        zu = _dot(n1, w_in_ref[:, D_FF + c * FF_CHUNK:D_FF + (c + 1) * FF_CHUNK])
        act_ref[:, c * FF_CHUNK:(c + 1) * FF_CHUNK] = (zg * _sigmoid(zg) * zu).astype(BF16)
    h2 = h1 + _dot(act_ref[...], w_out_ref[...])
    gate = _sigmoid(_dot(_rms(h2, g_ple_ref[...]).astype(BF16), wpg_ref[...]))
    out = h2 + gate * proj
    if final_norm:
        out = _rms(out, g_final_ref[...])
    o_ref[...] = out


def _tail(h, a, r, p, layer, wo, g_ffn, w_ffn_in, w_ffn_out, g_ple, wpg, wpp, g_final, final_norm):
    t = h.shape[0]
    steps = t // ROW_TILE
    row = lambda w: pl.BlockSpec((ROW_TILE, w), lambda i: (i, 0))
    return pl.pallas_call(
        functools.partial(_tail_kernel, final_norm=final_norm),
        grid=(steps,),
        in_specs=[row(D_MODEL), row(ATTN_WIDTH), row(RET_WIDTH),
                  pl.BlockSpec((ROW_TILE, PLE_DIM), lambda i: (layer * steps + i, 0)),
                  _resident((ATTN_WIDTH + RET_WIDTH, D_MODEL), layer),
                  _resident((1, D_MODEL), layer), _resident((D_MODEL, 2 * D_FF), layer),
                  _resident((D_FF, D_MODEL), layer), _resident((1, D_MODEL), layer),
                  _resident((D_MODEL, D_MODEL), layer), _resident((PLE_DIM, D_MODEL), layer),
                  _resident((1, D_MODEL))],
        out_specs=row(D_MODEL),
        out_shape=jax.ShapeDtypeStruct((t, D_MODEL), F32),
        scratch_shapes=[pltpu.VMEM((ROW_TILE, D_FF), BF16)],
        compiler_params=_params("parallel"),
        name="tail",
    )(h, a, r, p, wo, g_ffn, w_ffn_in, w_ffn_out, g_ple, wpg, wpp, g_final)


def kernel(x, p, attn_norm_g, w_in, ret_norm_g, w_out, ffn_norm_g, w_ffn_in, w_ffn_out,
           ple_norm_g, w_ple_gate, w_ple_proj, final_norm_g):
    b, s, d = x.shape
    depth = p.shape[0]
    assert d == D_MODEL and s % MOBA_BLOCK == 0 and s >= (MOBA_TOPK + 1) * MOBA_BLOCK
    assert (b * s) % ROW_TILE == 0 and s // MOBA_BLOCK <= LANES - ATTN_HEAD_DIM
    t = b * s
    h = x.reshape(t, d)
    gains = lambda g: g.reshape(depth, 1, -1).astype(F32)
    bf16 = lambda w: w.astype(BF16)
    attn_g, ffn_g, ple_g = gains(attn_norm_g), gains(ffn_norm_g), gains(ple_norm_g)
    w_in, w_out, w_ffn_in, w_ffn_out = bf16(w_in), bf16(w_out), bf16(w_ffn_in), bf16(w_ffn_out)
    w_ple_gate, w_ple_proj = bf16(w_ple_gate), bf16(w_ple_proj)
    p_rows = p.reshape(depth * t, PLE_DIM)
    for i in range(depth):
        u = _in_proj(h, attn_g, w_in, i).reshape(b, s, IN_WIDTH)
        a = _moba(u).reshape(t, ATTN_WIDTH)
        r = _retention(u, ret_norm_g[i].astype(F32)).reshape(t, RET_WIDTH)
        h = _tail(h, a, r, p_rows, i, w_out, ffn_g, w_ffn_in, w_ffn_out, ple_g, w_ple_gate,
                  w_ple_proj, final_norm_g.reshape(1, -1).astype(F32),
                  final_norm=(i == depth - 1))
    return h.reshape(b, s, d)
```

```python
import functools

import jax
import jax.numpy as jnp
from jax import lax
from jax.experimental import pallas as pl
from jax.experimental.pallas import tpu as pltpu

D_MODEL = 1024
PLE_DIM = 256
ATTN_WIDTH = 512
ATTN_HEAD_DIM = 64
RET_WIDTH = 512
RET_HEADS = 4
RET_HEAD_DIM = 128
IN_WIDTH = 3 * ATTN_WIDTH + 4 * RET_WIDTH
MOBA_BLOCK = 256
MOBA_TOPK = 3
RET_CHUNK = 256
ROPE_BASE = 10000.0
D_FF = 2816
EPS = 1e-6

LANES = 128
FF_CHUNK = 256
N_FF_CHUNKS = D_FF // FF_CHUNK
ROW_TILE = 512
COL_GROUP = 512
MASKED = -1e30
LOG2_E = 1.4426950408889634
VMEM_LIMIT = 52 * 1024 * 1024

F32 = jnp.float32
BF16 = jnp.bfloat16


def _dot(a, b):
    return jnp.dot(a, b, preferred_element_type=F32)


def _dot_nt(a, b):
    return lax.dot_general(a, b, (((1,), (1,)), ((), ())), preferred_element_type=F32)


def _rms(x, g):
    return x * lax.rsqrt(jnp.mean(x * x, axis=-1, keepdims=True) + EPS) * g


def _sigmoid(x):
    return 1.0 / (1.0 + jnp.exp(-x))


def _resident(shape, layer=None):
    zeros = (0,) * len(shape)
    if layer is None:
        return pl.BlockSpec(shape, lambda *_: zeros, pipeline_mode=pl.Buffered(1))
    return pl.BlockSpec((None,) + tuple(shape), lambda *_: (layer,) + zeros,
                        pipeline_mode=pl.Buffered(1))


def _params(*sem):
    return pltpu.CompilerParams(dimension_semantics=sem, vmem_limit_bytes=VMEM_LIMIT)


def _in_ret_kernel(x_ref, g_ref, w_ref, cos_ref, sin_ref, dmask_ref, kdec_ref, qdec_ref, gch_ref,
                   gn_ref, u_ref, r_ref, state_ref, *, tiles_per_seq):
    @pl.when(pl.program_id(0) % tiles_per_seq == 0)
    def _():
        state_ref[...] = jnp.zeros_like(state_ref)

    n = _rms(x_ref[...], g_ref[...]).astype(BF16)
    proj = lambda c: _dot(n, w_ref[:, c * COL_GROUP:(c + 1) * COL_GROUP])
    n_attn = 3
    rq, rk, rv, rg = (proj(n_attn + j) for j in range(4))

    lane = lax.broadcasted_iota(jnp.int32, (RET_CHUNK, LANES), 1)
    even = (lane & 1) == 0

    def rot(x, cos, sin):
        partner = jnp.where(even, pltpu.roll(x, LANES - 1, 1), pltpu.roll(x, 1, 1))
        return x * cos + partner * sin

    heads = range(RET_HEADS)
    col = lambda h: slice(h * RET_HEAD_DIM, (h + 1) * RET_HEAD_DIM)
    state = [state_ref[h] for h in heads]
    n_chunks = ROW_TILE // RET_CHUNK
    for c in range(max(n_chunks, n_attn)):
        if c < n_attn:
            u_ref[:, c * COL_GROUP:(c + 1) * COL_GROUP] = proj(c).astype(BF16)
        if c >= n_chunks:
            continue
        rows = slice(c * RET_CHUNK, (c + 1) * RET_CHUNK)
        cos, sin = cos_ref[rows, :], sin_ref[rows, :]
        q = [rot(rq[rows, col(h)], cos, sin) for h in heads]
        k = [rot(rk[rows, col(h)], cos, sin) * (RET_HEAD_DIM ** -0.5) for h in heads]
        v = [rv[rows, col(h)].astype(BF16) for h in heads]
        scores = [_dot_nt(q[h].astype(BF16), k[h].astype(BF16)) for h in heads]
        cross = [_dot((q[h] * qdec_ref[h]).astype(BF16), state[h].astype(BF16)) for h in heads]
        k_dec = [(k[h] * kdec_ref[h]).T.astype(BF16) for h in heads]
        state = [state[h] * gch_ref[h, 0:1, :] + _dot(k_dec[h], v[h]) for h in heads]
        y = [_dot((scores[h] * dmask_ref[h]).astype(BF16), v[h]) + cross[h] for h in heads]
        for h in heads:
            gate = rg[rows, col(h)]
            r_ref[rows, col(h)] = (gate * _sigmoid(gate) * _rms(y[h], gn_ref[h])).astype(BF16)
    for h in heads:
        state_ref[h] = state[h]


def _in_proj_retention(h, g, w, ret_norm_g, layer, seq_len):
    t = h.shape[0]
    c = RET_CHUNK
    nh = RET_HEADS
    assert ATTN_WIDTH == RET_WIDTH == COL_GROUP and seq_len % ROW_TILE == 0 and ROW_TILE % c == 0
    tiles_per_seq = seq_len // ROW_TILE

    inv = 1.0 / (ROPE_BASE ** jnp.linspace(0.0, 1.0, RET_HEAD_DIM // 2, dtype=F32))
    ang = jnp.arange(seq_len)[:, None].astype(F32) * inv[None, :]
    cos_t = jnp.repeat(jnp.cos(ang), 2, axis=-1)
    sin_t = jnp.repeat(jnp.sin(ang), 2, axis=-1) * jnp.tile(jnp.array([-1.0, 1.0], F32), RET_HEAD_DIM // 2)
    log_g = jnp.log1p(-jnp.exp2(-5.0 - jnp.arange(nh, dtype=F32)))
    i = jnp.arange(c, dtype=F32)
    rel = i[:, None] - i[None, :]
    dmask = jnp.where(rel[None] >= 0, jnp.exp(jnp.maximum(rel, 0.0)[None] * log_g[:, None, None]), 0.0)
    k_dec = jnp.exp((c - 1 - i)[None, :] * log_g[:, None])
    q_dec = jnp.exp((i + 1.0)[None, :] * log_g[:, None])
    g_chunk = jnp.exp(c * log_g)
    k_dec = jnp.broadcast_to(k_dec[:, :, None], (nh, c, LANES))
    q_dec = jnp.broadcast_to(q_dec[:, :, None], (nh, c, LANES))
    g_chunk = jnp.broadcast_to(g_chunk[:, None, None], (nh, 8, LANES))

    row = lambda width: pl.BlockSpec((ROW_TILE, width), lambda i: (i, 0))
    pos = pl.BlockSpec((ROW_TILE, LANES), lambda i: (i % tiles_per_seq, 0))
    return pl.pallas_call(
        functools.partial(_in_ret_kernel, tiles_per_seq=tiles_per_seq),
        grid=(t // ROW_TILE,),
        in_specs=[
            row(D_MODEL), _resident((1, D_MODEL), layer), _resident((D_MODEL, IN_WIDTH), layer),
            pos, pos,
            _resident((nh, c, c)), _resident((nh, c, LANES)), _resident((nh, c, LANES)),
            _resident((nh, 8, LANES)), _resident((nh, 1, RET_HEAD_DIM), layer),
        ],
        out_specs=[row(3 * ATTN_WIDTH), row(RET_WIDTH)],
        out_shape=[jax.ShapeDtypeStruct((t, 3 * ATTN_WIDTH), BF16),
                   jax.ShapeDtypeStruct((t, RET_WIDTH), BF16)],
        scratch_shapes=[pltpu.VMEM((nh, RET_HEAD_DIM, RET_HEAD_DIM), F32)],
        compiler_params=_params("arbitrary"),
        name="in_proj_retention",
    )(h, g, w, cos_t, sin_t, dmask, k_dec, q_dec, g_chunk, ret_norm_g)


ONES_ROWS = 16
VT_ROWS = ATTN_HEAD_DIM + ONES_ROWS
Q_BLOCKS = 4
Q_TILE = Q_BLOCKS * MOBA_BLOCK
PIPE_COLS = 2 * MOBA_BLOCK

def _moba_select(gate, q_block):
    nb = gate.shape[0]
    row = lax.broadcasted_iota(jnp.int32, gate.shape, 0)
    past = row < q_block
    g = jnp.where(past, gate, -jnp.inf)
    chosen = row == q_block
    for _ in range(MOBA_TOPK):
        top = jnp.max(g, axis=0, keepdims=True)
        hit = row == jnp.min(jnp.where(g == top, row, nb), axis=0, keepdims=True)
        chosen = chosen | (hit & past)
        g = jnp.where(hit, -jnp.inf, g)
    return jnp.where(chosen, 0.0, MASKED)


def _moba_kernel(q_ref, k_ref, v_ref, o_ref, kaug0_ref, kaug1_ref, vt0_ref, vt1_ref, kmean_ref,
                 sa_ref, sb_ref, *, nb):
    t = pl.program_id(2)
    blk = MOBA_BLOCK
    hd = ATTN_HEAD_DIM

    @pl.when(t == 0)
    def _():
        lane = lax.broadcasted_iota(jnp.int32, (blk, LANES), 1)
        head0_lane = lane < hd
        lane1 = lax.broadcasted_iota(jnp.int32, (1, LANES), 1)
        ones = jnp.ones((ONES_ROWS, blk), BF16)
        for n in range(nb):
            rows = slice(n * blk, (n + 1) * blk)
            kb = k_ref[0, rows, :]
            mean = jnp.sum(kb.astype(F32), axis=0, keepdims=True) * (1.0 / blk)
            kmean_ref[n:n + 1, :] = jnp.where(lane1 < hd, mean, 0.0)
            kmean_ref[nb + n:nb + n + 1, :] = jnp.where(lane1 < hd, 0.0, mean)
            kaug0_ref[n] = jnp.where(head0_lane, kb, (lane == hd + n).astype(BF16))
            kaug1_ref[n] = jnp.where(head0_lane, (lane == n).astype(BF16), kb)
            vt = v_ref[0, rows, :].astype(F32).T.astype(BF16)
            vt0_ref[n] = jnp.concatenate([vt[:hd], ones], axis=0)
            vt1_ref[n] = jnp.concatenate([vt[hd:], ones], axis=0)

    qt = (q_ref[0].astype(F32) * (hd ** -0.5 * LOG2_E)).T.astype(BF16)
    km = kmean_ref[...]
    km_hi = km.astype(BF16)
    km_lo = (km - km_hi.astype(F32)).astype(BF16)
    gate = _dot(km_hi, qt) + _dot(km_lo, qt)
    first = Q_BLOCKS * t
    zeros = lambda r: jnp.zeros((r, Q_TILE), BF16)
    kaug_refs = (kaug0_ref, kaug1_ref)
    vt_refs = (vt0_ref, vt1_ref)

    kpos = lax.broadcasted_iota(jnp.int32, (blk, blk), 0)
    qpos = lax.broadcasted_iota(jnp.int32, (blk, blk), 1)
    causal = kpos <= qpos
    group = lambda j: slice(j * blk, (j + 1) * blk)

    def softmax_pv(s, m_ref, vts):
        p = jnp.exp2(s - m_ref).astype(BF16)
        w = s.shape[1] // len(vts)
        parts = [_dot(vt, p[:, j * w:(j + 1) * w]) for j, vt in enumerate(vts)]
        return parts[0] if len(parts) == 1 else jnp.concatenate(parts, axis=1)

    qt_own = (jnp.concatenate([qt[:hd], zeros(LANES - hd)], axis=0),
              jnp.concatenate([zeros(hd), qt[hd:]], axis=0))
    for head in range(2):
        for j in range(Q_BLOCKS):
            logits = _dot(kaug_refs[head][first + j], qt_own[head][:, group(j)])
            sa_ref[head, :, group(j)] = jnp.where(causal, logits, MASKED)

    lane = lax.broadcasted_iota(jnp.int32, (1, Q_TILE), 1)
    q_block = first + sum((lane >= j * blk).astype(jnp.int32) for j in range(1, Q_BLOCKS))
    mask0 = _moba_select(gate[:nb], q_block).astype(BF16)
    mask1 = _moba_select(gate[nb:], q_block).astype(BF16)
    qt_sel = (jnp.concatenate([qt[:hd], mask0, zeros(LANES - hd - nb)], axis=0),
              jnp.concatenate([mask1, zeros(hd - nb), qt[hd:]], axis=0))

    def chunks(skip):
        width = Q_TILE - skip * blk
        return [(a, min(a + PIPE_COLS, width)) for a in range(0, width, PIPE_COLS)]

    def issue_chunk(s_ref, n, head, skip, a, b_):
        q0 = skip * blk
        s_ref[head, :, a:b_] = _dot(kaug_refs[head][n], qt_sel[head][:, q0 + a:q0 + b_])

    def issue_head(s_ref, n, head, skip=0):
        for a, b_ in chunks(skip):
            issue_chunk(s_ref, n, head, skip, a, b_)

    def consume_chunk(s_ref, n, head, m, acc, skip, a, b_):
        q0 = skip * blk
        s = s_ref[head, :, a:b_]
        m_old = m[:, q0 + a:q0 + b_]
        m_new = jnp.maximum(m_old, jnp.max(s, axis=0, keepdims=True))
        return m_new, (acc[:, q0 + a:q0 + b_] * jnp.exp2(m_old - m_new)
                       + softmax_pv(s, m_new, [vt_refs[head][n]]))

    def step(dst_ref, n_issue, src_ref, n_consume, carry, issue_skip=0, consume_skip=0):
        out = ()
        for head in range(2):
            m, acc = carry[2 * head], carry[2 * head + 1]
            q0 = consume_skip * blk
            m_cols = [m[:, :q0]] if q0 else []
            acc_cols = [acc[:, :q0]] if q0 else []
            todo_issue, todo_consume = chunks(issue_skip), chunks(consume_skip)
            for k in range(max(len(todo_issue), len(todo_consume))):
                if k < len(todo_issue):
                    issue_chunk(dst_ref, n_issue, head, issue_skip, *todo_issue[k])
                if k < len(todo_consume):
                    m_new, acc_new = consume_chunk(src_ref, n_consume, head, m, acc, consume_skip,
                                                   *todo_consume[k])
                    m_cols.append(m_new)
                    acc_cols.append(acc_new)
            out += (jnp.concatenate(m_cols, axis=1), jnp.concatenate(acc_cols, axis=1))
        return out

    bufs = (sb_ref, sa_ref)
    for head in range(2):
        issue_head(sb_ref, first, head, skip=1)
    carry = ()
    for head in range(2):
        s = sa_ref[head]
        m = jnp.max(s, axis=0, keepdims=True)
        carry += (m, softmax_pv(s, m, [vt_refs[head][first + j] for j in range(Q_BLOCKS)]))
    for j in range(Q_BLOCKS - 1):
        if j + 1 < Q_BLOCKS - 1:
            carry = step(bufs[(j + 1) % 2], first + j + 1, bufs[j % 2], first + j, carry,
                         issue_skip=j + 2, consume_skip=j + 1)
        else:
            carry = step(bufs[(j + 1) % 2], 0, bufs[j % 2], first + j, carry, consume_skip=j + 1)

    def two_blocks(n, carry):
        carry = step(sb_ref, n + 1, sa_ref, n, carry)
        return step(sa_ref, n + 2, sb_ref, n + 1, carry)

    carry = lax.fori_loop(
        0, first // 4, lambda j, c: two_blocks(4 * j + 2, two_blocks(4 * j, c)), carry)
    if Q_BLOCKS % 4:
        carry = lax.fori_loop(0, (first % 4) // 2, lambda _, c: two_blocks(first - 2, c), carry)
    _, a0, _, a1 = carry
    out_t = jnp.concatenate([a0[:hd] / a0[hd:hd + 1], a1[:hd] / a1[hd:hd + 1]], axis=0)
    o_ref[0] = out_t.T.astype(BF16)


def _moba(u):
    b, s, _ = u.shape
    nb = s // MOBA_BLOCK
    assert nb % ONES_ROWS == 0 and nb <= ATTN_HEAD_DIM and s % Q_TILE == 0 and Q_BLOCKS % 2 == 0
    pairs = ATTN_WIDTH // LANES
    return pl.pallas_call(
        functools.partial(_moba_kernel, nb=nb),
        grid=(b, pairs, s // Q_TILE),
        in_specs=[
            pl.BlockSpec((1, Q_TILE, LANES), lambda bi, hp, t: (bi, t, hp)),
            pl.BlockSpec((1, s, LANES), lambda bi, hp, t: (bi, 0, pairs + hp)),
            pl.BlockSpec((1, s, LANES), lambda bi, hp, t: (bi, 0, 2 * pairs + hp)),
        ],
        out_specs=pl.BlockSpec((1, Q_TILE, LANES), lambda bi, hp, t: (bi, t, hp)),
        out_shape=jax.ShapeDtypeStruct((b, s, ATTN_WIDTH), BF16),
        scratch_shapes=[
            pltpu.VMEM((nb, MOBA_BLOCK, LANES), BF16),
            pltpu.VMEM((nb, MOBA_BLOCK, LANES), BF16),
            pltpu.VMEM((nb, VT_ROWS, MOBA_BLOCK), BF16),
            pltpu.VMEM((nb, VT_ROWS, MOBA_BLOCK), BF16),
            pltpu.VMEM((2 * nb, LANES), F32),
            pltpu.VMEM((2, MOBA_BLOCK, Q_TILE), F32),
            pltpu.VMEM((2, MOBA_BLOCK, Q_TILE), F32),
        ],
        compiler_params=_params("parallel", "parallel", "arbitrary"),
        name="moba",
    )(u, u, u)


def _tail_kernel(h_ref, a_ref, r_ref, p_ref, wo_ref, g_ffn_ref, w_in_ref, w_out_ref,
                 g_ple_ref, wpg_ref, wpp_ref, g_final_ref, o_ref, act_ref, *, final_norm):
    h1 = (h_ref[...] + _dot(a_ref[...], wo_ref[:ATTN_WIDTH, :])
          + _dot(r_ref[...], wo_ref[ATTN_WIDTH:, :]))
    n1 = _rms(h1, g_ffn_ref[...]).astype(BF16)
    proj = _dot(p_ref[...].astype(BF16), wpp_ref[...])
    for c in range(N_FF_CHUNKS):
        zg = _dot(n1, w_in_ref[:, c * FF_CHUNK:(c + 1) * FF_CHUNK])
        zu = _dot(n1, w_in_ref[:, D_FF + c * FF_CHUNK:D_FF + (c + 1) * FF_CHUNK])
        act_ref[:, c * FF_CHUNK:(c + 1) * FF_CHUNK] = (zg * _sigmoid(zg) * zu).astype(BF16)
    h2 = h1 + _dot(act_ref[...], w_out_ref[...])
    gate = _sigmoid(_dot(_rms(h2, g_ple_ref[...]).astype(BF16), wpg_ref[...]))
    out = h2 + gate * proj
    if final_norm:
        out = _rms(out, g_final_ref[...])
    o_ref[...] = out


def _tail(h, a, r, p, layer, wo, g_ffn, w_ffn_in, w_ffn_out, g_ple, wpg, wpp, g_final, final_norm):
    t = h.shape[0]
    steps = t // ROW_TILE
    row = lambda w: pl.BlockSpec((ROW_TILE, w), lambda i: (i, 0))
    return pl.pallas_call(
        functools.partial(_tail_kernel, final_norm=final_norm),
        grid=(steps,),
        in_specs=[row(D_MODEL), row(ATTN_WIDTH), row(RET_WIDTH),
                  pl.BlockSpec((ROW_TILE, PLE_DIM), lambda i: (layer * steps + i, 0)),
                  _resident((ATTN_WIDTH + RET_WIDTH, D_MODEL), layer),
                  _resident((1, D_MODEL), layer), _resident((D_MODEL, 2 * D_FF), layer),
                  _resident((D_FF, D_MODEL), layer), _resident((1, D_MODEL), layer),
                  _resident((D_MODEL, D_MODEL), layer), _resident((PLE_DIM, D_MODEL), layer),
                  _resident((1, D_MODEL))],
        out_specs=row(D_MODEL),
        out_shape=jax.ShapeDtypeStruct((t, D_MODEL), F32),
        scratch_shapes=[pltpu.VMEM((ROW_TILE, D_FF), BF16)],
        compiler_params=_params("parallel"),
        name="tail",
    )(h, a, r, p, wo, g_ffn, w_ffn_in, w_ffn_out, g_ple, wpg, wpp, g_final)


def kernel(x, p, attn_norm_g, w_in, ret_norm_g, w_out, ffn_norm_g, w_ffn_in, w_ffn_out,
           ple_norm_g, w_ple_gate, w_ple_proj, final_norm_g):
    b, s, d = x.shape
    depth = p.shape[0]
    assert d == D_MODEL and s % MOBA_BLOCK == 0 and s >= (MOBA_TOPK + 1) * MOBA_BLOCK
    assert (b * s) % ROW_TILE == 0 and s // MOBA_BLOCK <= LANES - ATTN_HEAD_DIM
    t = b * s
    h = x.reshape(t, d)
    gains = lambda g: g.reshape(depth, 1, -1).astype(F32)
    bf16 = lambda w: w.astype(BF16)
    attn_g, ffn_g, ple_g = gains(attn_norm_g), gains(ffn_norm_g), gains(ple_norm_g)
    w_in, w_out, w_ffn_in, w_ffn_out = bf16(w_in), bf16(w_out), bf16(w_ffn_in), bf16(w_ffn_out)
    w_ple_gate, w_ple_proj = bf16(w_ple_gate), bf16(w_ple_proj)
    ret_g = ret_norm_g.reshape(depth, RET_HEADS, 1, RET_HEAD_DIM).astype(F32)
    p_rows = p.reshape(depth * t, PLE_DIM)
    for i in range(depth):
        u, r = _in_proj_retention(h, attn_g, w_in, ret_g, i, s)
        a = _moba(u.reshape(b, s, 3 * ATTN_WIDTH)).reshape(t, ATTN_WIDTH)
        h = _tail(h, a, r, p_rows, i, w_out, ffn_g, w_ffn_in, w_ffn_out, ple_g, w_ple_gate,
                  w_ple_proj, final_norm_g.reshape(1, -1).astype(F32),
                  final_norm=(i == depth - 1))
    return h.reshape(b, s, d)
```

```python
import functools

import jax
import jax.numpy as jnp
from jax import lax
from jax.experimental import pallas as pl
from jax.experimental.pallas import tpu as pltpu

D_MODEL = 1024
PLE_DIM = 256
ATTN_WIDTH = 512
ATTN_HEAD_DIM = 64
RET_WIDTH = 512
RET_HEADS = 4
RET_HEAD_DIM = 128
IN_WIDTH = 3 * ATTN_WIDTH + 4 * RET_WIDTH
MOBA_BLOCK = 256
MOBA_TOPK = 3
RET_CHUNK = 256
ROPE_BASE = 10000.0
D_FF = 2816
EPS = 1e-6

LANES = 128
FF_CHUNK = 256
N_FF_CHUNKS = D_FF // FF_CHUNK
ROW_TILE = 512
COL_GROUP = 512
MASKED = -1e30
LOG2_E = 1.4426950408889634
VMEM_LIMIT = 52 * 1024 * 1024

F32 = jnp.float32
BF16 = jnp.bfloat16


def _dot(a, b):
    return jnp.dot(a, b, preferred_element_type=F32)


def _dot_nt(a, b):
    return lax.dot_general(a, b, (((1,), (1,)), ((), ())), preferred_element_type=F32)


def _rms(x, g):
    return x * lax.rsqrt(jnp.mean(x * x, axis=-1, keepdims=True) + EPS) * g


def _sigmoid(x):
    return 1.0 / (1.0 + jnp.exp(-x))


def _resident(shape, layer=None):
    zeros = (0,) * len(shape)
    if layer is None:
        return pl.BlockSpec(shape, lambda *_: zeros, pipeline_mode=pl.Buffered(1))
    return pl.BlockSpec((None,) + tuple(shape), lambda *_: (layer,) + zeros,
                        pipeline_mode=pl.Buffered(1))


def _params(*sem):
    return pltpu.CompilerParams(dimension_semantics=sem, vmem_limit_bytes=VMEM_LIMIT)


BF16_SUBLANES = 16


class _Cast:
    def __init__(self, w, layer, steps):
        rows, cols = w.shape[1:]
        per = -(-rows // steps)
        per = -(-per // BF16_SUBLANES) * BF16_SUBLANES
        while rows % per:
            per += BF16_SUBLANES
        self.src = w
        self.active = rows // per
        last = self.active - 1
        self.in_spec = pl.BlockSpec((None, per, cols), lambda i: (layer, jnp.minimum(i, last), 0))
        self.out_spec = pl.BlockSpec((per, cols), lambda i: (jnp.minimum(i, last), 0))
        self.out_shape = jax.ShapeDtypeStruct((rows, cols), BF16)


def _run_casts(step, steps, srcs, dsts, actives, every_step):
    for src, dst, active in zip(srcs, dsts, actives):
        if (active == steps) != every_step:
            continue
        if every_step:
            dst[...] = src[...].astype(BF16)
        else:
            @pl.when(step < active)
            def _():
                dst[...] = src[...].astype(BF16)


def _in_ret_kernel(*refs, tiles_per_seq, steps, cast_actives):
    n_cast = len(cast_actives)
    (x_ref, g_ref, w_ref, cos_ref, sin_ref, dmask_ref, kdec_ref, qdec_ref, gch_ref,
     gn_ref) = refs[:10]
    cast_srcs = refs[10:10 + n_cast]
    u_ref, r_ref = refs[10 + n_cast:12 + n_cast]
    cast_dsts = refs[12 + n_cast:12 + 2 * n_cast]
    state_ref = refs[12 + 2 * n_cast]
    step = pl.program_id(0)

    @pl.when(step % tiles_per_seq == 0)
    def _():
        state_ref[...] = jnp.zeros_like(state_ref)

    _run_casts(step, steps, cast_srcs, cast_dsts, cast_actives, every_step=True)
    n = _rms(x_ref[...], g_ref[...]).astype(BF16)
    proj = lambda c: _dot(n, w_ref[:, c * COL_GROUP:(c + 1) * COL_GROUP])
    n_attn = 3
    rq, rk, rv, rg = (proj(n_attn + j) for j in range(4))

    lane = lax.broadcasted_iota(jnp.int32, (RET_CHUNK, LANES), 1)
    even = (lane & 1) == 0

    def rot(x, cos, sin):
        partner = jnp.where(even, pltpu.roll(x, LANES - 1, 1), pltpu.roll(x, 1, 1))
        return x * cos + partner * sin

    heads = range(RET_HEADS)
    col = lambda h: slice(h * RET_HEAD_DIM, (h + 1) * RET_HEAD_DIM)
    state = [state_ref[h] for h in heads]
    n_chunks = ROW_TILE // RET_CHUNK
    for c in range(max(n_chunks, n_attn)):
        if c < n_attn:
            u_ref[:, c * COL_GROUP:(c + 1) * COL_GROUP] = proj(c).astype(BF16)
        if c >= n_chunks:
            continue
        rows = slice(c * RET_CHUNK, (c + 1) * RET_CHUNK)
        cos, sin = cos_ref[rows, :], sin_ref[rows, :]
        q = [rot(rq[rows, col(h)], cos, sin) for h in heads]
        k = [rot(rk[rows, col(h)], cos, sin) * (RET_HEAD_DIM ** -0.5) for h in heads]
        v = [rv[rows, col(h)].astype(BF16) for h in heads]
        scores = [_dot_nt(q[h].astype(BF16), k[h].astype(BF16)) for h in heads]
        cross = [_dot((q[h] * qdec_ref[h]).astype(BF16), state[h].astype(BF16)) for h in heads]
        k_dec = [(k[h] * kdec_ref[h]).T.astype(BF16) for h in heads]
        state = [state[h] * gch_ref[h, 0:1, :] + _dot(k_dec[h], v[h]) for h in heads]
        y = [_dot((scores[h] * dmask_ref[h]).astype(BF16), v[h]) + cross[h] for h in heads]
        for h in heads:
            gate = rg[rows, col(h)]
            r_ref[rows, col(h)] = (gate * _sigmoid(gate) * _rms(y[h], gn_ref[h])).astype(BF16)
    for h in heads:
        state_ref[h] = state[h]
    _run_casts(step, steps, cast_srcs, cast_dsts, cast_actives, every_step=False)


def _in_proj_retention(h, g, w, ret_norm_g, layer, seq_len, to_bf16):
    t = h.shape[0]
    steps = t // ROW_TILE
    casts = [_Cast(w_f32, layer, steps) for w_f32 in to_bf16]
    c = RET_CHUNK
    nh = RET_HEADS
    assert ATTN_WIDTH == RET_WIDTH == COL_GROUP and seq_len % ROW_TILE == 0 and ROW_TILE % c == 0
    tiles_per_seq = seq_len // ROW_TILE

    inv = 1.0 / (ROPE_BASE ** jnp.linspace(0.0, 1.0, RET_HEAD_DIM // 2, dtype=F32))
    ang = jnp.arange(seq_len)[:, None].astype(F32) * inv[None, :]
    cos_t = jnp.repeat(jnp.cos(ang), 2, axis=-1)
    sin_t = jnp.repeat(jnp.sin(ang), 2, axis=-1) * jnp.tile(jnp.array([-1.0, 1.0], F32), RET_HEAD_DIM // 2)
    log_g = jnp.log1p(-jnp.exp2(-5.0 - jnp.arange(nh, dtype=F32)))
    i = jnp.arange(c, dtype=F32)
    rel = i[:, None] - i[None, :]
    dmask = jnp.where(rel[None] >= 0, jnp.exp(jnp.maximum(rel, 0.0)[None] * log_g[:, None, None]), 0.0)
    k_dec = jnp.exp((c - 1 - i)[None, :] * log_g[:, None])
    q_dec = jnp.exp((i + 1.0)[None, :] * log_g[:, None])
    g_chunk = jnp.exp(c * log_g)
    k_dec = jnp.broadcast_to(k_dec[:, :, None], (nh, c, LANES))
    q_dec = jnp.broadcast_to(q_dec[:, :, None], (nh, c, LANES))
    g_chunk = jnp.broadcast_to(g_chunk[:, None, None], (nh, 8, LANES))

    row = lambda width: pl.BlockSpec((ROW_TILE, width), lambda i: (i, 0))
    pos = pl.BlockSpec((ROW_TILE, LANES), lambda i: (i % tiles_per_seq, 0))
    u, r, *converted = pl.pallas_call(
        functools.partial(_in_ret_kernel, tiles_per_seq=tiles_per_seq, steps=steps,
                          cast_actives=tuple(cast.active for cast in casts)),
        grid=(steps,),
        in_specs=[
            row(D_MODEL), _resident((1, D_MODEL), layer), _resident((D_MODEL, IN_WIDTH)),
            pos, pos,
            _resident((nh, c, c)), _resident((nh, c, LANES)), _resident((nh, c, LANES)),
            _resident((nh, 8, LANES)), _resident((nh, 1, RET_HEAD_DIM), layer),
        ] + [cast.in_spec for cast in casts],
        out_specs=[row(3 * ATTN_WIDTH), row(RET_WIDTH)] + [cast.out_spec for cast in casts],
        out_shape=[jax.ShapeDtypeStruct((t, 3 * ATTN_WIDTH), BF16),
                   jax.ShapeDtypeStruct((t, RET_WIDTH), BF16)] + [cast.out_shape for cast in casts],
        scratch_shapes=[pltpu.VMEM((nh, RET_HEAD_DIM, RET_HEAD_DIM), F32)],
        compiler_params=_params("arbitrary"),
        name="in_proj_retention",
    )(h, g, w, cos_t, sin_t, dmask, k_dec, q_dec, g_chunk, ret_norm_g, *to_bf16)
    return u, r, converted


ONES_ROWS = 16
VT_ROWS = ATTN_HEAD_DIM + ONES_ROWS
Q_BLOCKS = 4
Q_TILE = Q_BLOCKS * MOBA_BLOCK
PIPE_COLS = 2 * MOBA_BLOCK

def _moba_select(gate, q_block):
    nb = gate.shape[0]
    row = lax.broadcasted_iota(jnp.int32, gate.shape, 0)
    past = row < q_block
    g = jnp.where(past, gate, -jnp.inf)
    chosen = row == q_block
    for _ in range(MOBA_TOPK):
        top = jnp.max(g, axis=0, keepdims=True)
        hit = row == jnp.min(jnp.where(g == top, row, nb), axis=0, keepdims=True)
        chosen = chosen | (hit & past)
        g = jnp.where(hit, -jnp.inf, g)
    return jnp.where(chosen, 0.0, MASKED)


def _moba_kernel(q_ref, k_ref, v_ref, o_ref, kaug0_ref, kaug1_ref, vt0_ref, vt1_ref, kmean_ref,
                 sa_ref, sb_ref, *, nb):
    t = pl.program_id(2)
    blk = MOBA_BLOCK
    hd = ATTN_HEAD_DIM

    @pl.when(t == 0)
    def _():
        lane = lax.broadcasted_iota(jnp.int32, (blk, LANES), 1)
        head0_lane = lane < hd
        lane1 = lax.broadcasted_iota(jnp.int32, (1, LANES), 1)
        ones = jnp.ones((ONES_ROWS, blk), BF16)
        for n in range(nb):
            rows = slice(n * blk, (n + 1) * blk)
            kb = k_ref[0, rows, :]
            mean = jnp.sum(kb.astype(F32), axis=0, keepdims=True) * (1.0 / blk)
            kmean_ref[n:n + 1, :] = jnp.where(lane1 < hd, mean, 0.0)
            kmean_ref[nb + n:nb + n + 1, :] = jnp.where(lane1 < hd, 0.0, mean)
            kaug0_ref[n] = jnp.where(head0_lane, kb, (lane == hd + n).astype(BF16))
            kaug1_ref[n] = jnp.where(head0_lane, (lane == n).astype(BF16), kb)
            vt = v_ref[0, rows, :].astype(F32).T.astype(BF16)
            vt0_ref[n] = jnp.concatenate([vt[:hd], ones], axis=0)
            vt1_ref[n] = jnp.concatenate([vt[hd:], ones], axis=0)

    qt = (q_ref[0].astype(F32) * (hd ** -0.5 * LOG2_E)).T.astype(BF16)
    km = kmean_ref[...]
    km_hi = km.astype(BF16)
    km_lo = (km - km_hi.astype(F32)).astype(BF16)
    gate = _dot(km_hi, qt) + _dot(km_lo, qt)
    first = Q_BLOCKS * t
    zeros = lambda r: jnp.zeros((r, Q_TILE), BF16)
    kaug_refs = (kaug0_ref, kaug1_ref)
    vt_refs = (vt0_ref, vt1_ref)

    kpos = lax.broadcasted_iota(jnp.int32, (blk, blk), 0)
    qpos = lax.broadcasted_iota(jnp.int32, (blk, blk), 1)
    causal = kpos <= qpos
    group = lambda j: slice(j * blk, (j + 1) * blk)

    def softmax_pv(s, m_ref, vts):
        p = jnp.exp2(s - m_ref).astype(BF16)
        w = s.shape[1] // len(vts)
        parts = [_dot(vt, p[:, j * w:(j + 1) * w]) for j, vt in enumerate(vts)]
        return parts[0] if len(parts) == 1 else jnp.concatenate(parts, axis=1)

    qt_own = (jnp.concatenate([qt[:hd], zeros(LANES - hd)], axis=0),
              jnp.concatenate([zeros(hd), qt[hd:]], axis=0))
    for head in range(2):
        for j in range(Q_BLOCKS):
            logits = _dot(kaug_refs[head][first + j], qt_own[head][:, group(j)])
            sa_ref[head, :, group(j)] = jnp.where(causal, logits, MASKED)

    lane = lax.broadcasted_iota(jnp.int32, (1, Q_TILE), 1)
    q_block = first + sum((lane >= j * blk).astype(jnp.int32) for j in range(1, Q_BLOCKS))
    mask0 = _moba_select(gate[:nb], q_block).astype(BF16)
    mask1 = _moba_select(gate[nb:], q_block).astype(BF16)
    qt_sel = (jnp.concatenate([qt[:hd], mask0, zeros(LANES - hd - nb)], axis=0),
              jnp.concatenate([mask1, zeros(hd - nb), qt[hd:]], axis=0))

    def chunks(skip):
        width = Q_TILE - skip * blk
        return [(a, min(a + PIPE_COLS, width)) for a in range(0, width, PIPE_COLS)]

    def issue_chunk(s_ref, n, head, skip, a, b_):
        q0 = skip * blk
        s_ref[head, :, a:b_] = _dot(kaug_refs[head][n], qt_sel[head][:, q0 + a:q0 + b_])

    def issue_head(s_ref, n, head, skip=0):
        for a, b_ in chunks(skip):
            issue_chunk(s_ref, n, head, skip, a, b_)

    def consume_chunk(s_ref, n, head, m, acc, skip, a, b_):
        q0 = skip * blk
        s = s_ref[head, :, a:b_]
        m_old = m[:, q0 + a:q0 + b_]
        m_new = jnp.maximum(m_old, jnp.max(s, axis=0, keepdims=True))
        return m_new, (acc[:, q0 + a:q0 + b_] * jnp.exp2(m_old - m_new)
                       + softmax_pv(s, m_new, [vt_refs[head][n]]))

    def step(dst_ref, n_issue, src_ref, n_consume, carry, issue_skip=0, consume_skip=0):
        out = ()
        for head in range(2):
            m, acc = carry[2 * head], carry[2 * head + 1]
            q0 = consume_skip * blk
            m_cols = [m[:, :q0]] if q0 else []
            acc_cols = [acc[:, :q0]] if q0 else []
            todo_issue, todo_consume = chunks(issue_skip), chunks(consume_skip)
            for k in range(max(len(todo_issue), len(todo_consume))):
                if k < len(todo_issue):
                    issue_chunk(dst_ref, n_issue, head, issue_skip, *todo_issue[k])
                if k < len(todo_consume):
                    m_new, acc_new = consume_chunk(src_ref, n_consume, head, m, acc, consume_skip,
                                                   *todo_consume[k])
                    m_cols.append(m_new)
                    acc_cols.append(acc_new)
            out += (jnp.concatenate(m_cols, axis=1), jnp.concatenate(acc_cols, axis=1))
        return out

    bufs = (sb_ref, sa_ref)
    for head in range(2):
        issue_head(sb_ref, first, head, skip=1)
    carry = ()
    for head in range(2):
        s = sa_ref[head]
        m = jnp.max(s, axis=0, keepdims=True)
        carry += (m, softmax_pv(s, m, [vt_refs[head][first + j] for j in range(Q_BLOCKS)]))
    for j in range(Q_BLOCKS - 1):
        if j + 1 < Q_BLOCKS - 1:
            carry = step(bufs[(j + 1) % 2], first + j + 1, bufs[j % 2], first + j, carry,
                         issue_skip=j + 2, consume_skip=j + 1)
        else:
            carry = step(bufs[(j + 1) % 2], 0, bufs[j % 2], first + j, carry, consume_skip=j + 1)

    def two_blocks(n, carry):
        carry = step(sb_ref, n + 1, sa_ref, n, carry)
        return step(sa_ref, n + 2, sb_ref, n + 1, carry)

    carry = lax.fori_loop(
        0, first // 4, lambda j, c: two_blocks(4 * j + 2, two_blocks(4 * j, c)), carry)
    if Q_BLOCKS % 4:
        carry = lax.fori_loop(0, (first % 4) // 2, lambda _, c: two_blocks(first - 2, c), carry)
    _, a0, _, a1 = carry
    out_t = jnp.concatenate([a0[:hd] / a0[hd:hd + 1], a1[:hd] / a1[hd:hd + 1]], axis=0)
    o_ref[0] = out_t.T.astype(BF16)


def _moba(u):
    b, s, _ = u.shape
    nb = s // MOBA_BLOCK
    assert nb % ONES_ROWS == 0 and nb <= ATTN_HEAD_DIM and s % Q_TILE == 0 and Q_BLOCKS % 2 == 0
    pairs = ATTN_WIDTH // LANES
    return pl.pallas_call(
        functools.partial(_moba_kernel, nb=nb),
        grid=(b, pairs, s // Q_TILE),
        in_specs=[
            pl.BlockSpec((1, Q_TILE, LANES), lambda bi, hp, t: (bi, t, hp)),
            pl.BlockSpec((1, s, LANES), lambda bi, hp, t: (bi, 0, pairs + hp)),
            pl.BlockSpec((1, s, LANES), lambda bi, hp, t: (bi, 0, 2 * pairs + hp)),
        ],
        out_specs=pl.BlockSpec((1, Q_TILE, LANES), lambda bi, hp, t: (bi, t, hp)),
        out_shape=jax.ShapeDtypeStruct((b, s, ATTN_WIDTH), BF16),
        scratch_shapes=[
            pltpu.VMEM((nb, MOBA_BLOCK, LANES), BF16),
            pltpu.VMEM((nb, MOBA_BLOCK, LANES), BF16),
            pltpu.VMEM((nb, VT_ROWS, MOBA_BLOCK), BF16),
            pltpu.VMEM((nb, VT_ROWS, MOBA_BLOCK), BF16),
            pltpu.VMEM((2 * nb, LANES), F32),
            pltpu.VMEM((2, MOBA_BLOCK, Q_TILE), F32),
            pltpu.VMEM((2, MOBA_BLOCK, Q_TILE), F32),
        ],
        compiler_params=_params("parallel", "parallel", "arbitrary"),
        name="moba",
    )(u, u, u)


def _tail_kernel(*refs, final_norm, steps, cast_actives):
    n_cast = len(cast_actives)
    (h_ref, a_ref, r_ref, p_ref, wo_ref, g_ffn_ref, w_in_ref, w_out_ref, g_ple_ref, wpg_ref,
     wpp_ref, g_final_ref) = refs[:12]
    cast_srcs = refs[12:12 + n_cast]
    o_ref = refs[12 + n_cast]
    cast_dsts = refs[13 + n_cast:13 + 2 * n_cast]
    act_ref = refs[13 + 2 * n_cast]
    step = pl.program_id(0)
    _run_casts(step, steps, cast_srcs, cast_dsts, cast_actives, every_step=True)
    h1 = (h_ref[...] + _dot(a_ref[...], wo_ref[:ATTN_WIDTH, :])
          + _dot(r_ref[...], wo_ref[ATTN_WIDTH:, :]))
    n1 = _rms(h1, g_ffn_ref[...]).astype(BF16)
    proj = _dot(p_ref[...].astype(BF16), wpp_ref[...])
    for c in range(N_FF_CHUNKS):
        zg = _dot(n1, w_in_ref[:, c * FF_CHUNK:(c + 1) * FF_CHUNK])
        zu = _dot(n1, w_in_ref[:, D_FF + c * FF_CHUNK:D_FF + (c + 1) * FF_CHUNK])
        act_ref[:, c * FF_CHUNK:(c + 1) * FF_CHUNK] = (zg * _sigmoid(zg) * zu).astype(BF16)
    h2 = h1 + _dot(act_ref[...], w_out_ref[...])
    gate = _sigmoid(_dot(_rms(h2, g_ple_ref[...]).astype(BF16), wpg_ref[...]))
    out = h2 + gate * proj
    if final_norm:
        out = _rms(out, g_final_ref[...])
    o_ref[...] = out
    _run_casts(step, steps, cast_srcs, cast_dsts, cast_actives, every_step=False)


def _tail(h, a, r, p, layer, wo, g_ffn, w_ffn_in, w_ffn_out, g_ple, wpg, wpp, g_final, final_norm,
          to_bf16):
    t = h.shape[0]
    steps = t // ROW_TILE
    casts = [_Cast(w_f32, cast_layer, steps) for w_f32, cast_layer in to_bf16]
    row = lambda w: pl.BlockSpec((ROW_TILE, w), lambda i: (i, 0))
    out, *converted = pl.pallas_call(
        functools.partial(_tail_kernel, final_norm=final_norm, steps=steps,
                          cast_actives=tuple(cast.active for cast in casts)),
        grid=(steps,),
        in_specs=[row(D_MODEL), row(ATTN_WIDTH), row(RET_WIDTH),
                  pl.BlockSpec((ROW_TILE, PLE_DIM), lambda i: (layer * steps + i, 0)),
                  _resident((ATTN_WIDTH + RET_WIDTH, D_MODEL)),
                  _resident((1, D_MODEL), layer), _resident((D_MODEL, 2 * D_FF)),
                  _resident((D_FF, D_MODEL)), _resident((1, D_MODEL), layer),
                  _resident((D_MODEL, D_MODEL)), _resident((PLE_DIM, D_MODEL)),
                  _resident((1, D_MODEL))] + [cast.in_spec for cast in casts],
        out_specs=[row(D_MODEL)] + [cast.out_spec for cast in casts],
        out_shape=[jax.ShapeDtypeStruct((t, D_MODEL), F32)] + [cast.out_shape for cast in casts],
        scratch_shapes=[pltpu.VMEM((ROW_TILE, D_FF), BF16)],
        compiler_params=_params("arbitrary"),
        name="tail",
    )(h, a, r, p, wo, g_ffn, w_ffn_in, w_ffn_out, g_ple, wpg, wpp, g_final,
      *[w_f32 for w_f32, _ in to_bf16])
    return out, converted


def kernel(x, p, attn_norm_g, w_in, ret_norm_g, w_out, ffn_norm_g, w_ffn_in, w_ffn_out,
           ple_norm_g, w_ple_gate, w_ple_proj, final_norm_g):
    b, s, d = x.shape
    depth = p.shape[0]
    assert d == D_MODEL and s % MOBA_BLOCK == 0 and s >= (MOBA_TOPK + 1) * MOBA_BLOCK
    assert (b * s) % ROW_TILE == 0 and s // MOBA_BLOCK <= LANES - ATTN_HEAD_DIM
    t = b * s
    h = x.reshape(t, d)
    gains = lambda g: g.reshape(depth, 1, -1).astype(F32)
    attn_g, ffn_g, ple_g = gains(attn_norm_g), gains(ffn_norm_g), gains(ple_norm_g)
    ret_g = ret_norm_g.reshape(depth, RET_HEADS, 1, RET_HEAD_DIM).astype(F32)
    p_rows = p.reshape(depth * t, PLE_DIM)
    tail_weights = [w_out, w_ffn_in, w_ffn_out, w_ple_gate, w_ple_proj]
    w_in_bf16 = w_in[0].astype(BF16)
    for i in range(depth):
        last = i == depth - 1
        u, r, tail_bf16 = _in_proj_retention(h, attn_g, w_in_bf16, ret_g, i, s, tail_weights)
        a = _moba(u.reshape(b, s, 3 * ATTN_WIDTH)).reshape(t, ATTN_WIDTH)
        wo, wf_in, wf_out, wpg, wpp = tail_bf16
        h, next_w_in = _tail(h, a, r, p_rows, i, wo, ffn_g, wf_in, wf_out, ple_g, wpg, wpp,
                             final_norm_g.reshape(1, -1).astype(F32), final_norm=last,
                             to_bf16=[] if last else [(w_in, i + 1)])
        if not last:
            (w_in_bf16,) = next_w_in
    return h.reshape(b, s, d)
```

```python
import functools

import jax
import jax.numpy as jnp
from jax import lax
from jax.experimental import pallas as pl
from jax.experimental.pallas import tpu as pltpu

D_MODEL = 1024
PLE_DIM = 256
ATTN_WIDTH = 512
ATTN_HEAD_DIM = 64
RET_WIDTH = 512
RET_HEADS = 4
RET_HEAD_DIM = 128
IN_WIDTH = 3 * ATTN_WIDTH + 4 * RET_WIDTH
MOBA_BLOCK = 256
MOBA_TOPK = 3
RET_CHUNK = 256
ROPE_BASE = 10000.0
D_FF = 2816
EPS = 1e-6

LANES = 128
FF_CHUNK = 256
N_FF_CHUNKS = D_FF // FF_CHUNK
ROW_TILE = 512
IN_ROW_TILE = 1024
COL_GROUP = 512
MASKED = -1e30
LOG2_E = 1.4426950408889634
VMEM_LIMIT = 52 * 1024 * 1024

F32 = jnp.float32
BF16 = jnp.bfloat16


def _dot(a, b):
    return jnp.dot(a, b, preferred_element_type=F32)


def _dot_nt(a, b):
    return lax.dot_general(a, b, (((1,), (1,)), ((), ())), preferred_element_type=F32)


def _rms(x, g):
    return x * lax.rsqrt(jnp.mean(x * x, axis=-1, keepdims=True) + EPS) * g


def _sigmoid(x):
    return 1.0 / (1.0 + jnp.exp(-x))


def _resident(shape, layer=None):
    zeros = (0,) * len(shape)
    if layer is None:
        return pl.BlockSpec(shape, lambda *_: zeros, pipeline_mode=pl.Buffered(1))
    return pl.BlockSpec((None,) + tuple(shape), lambda *_: (layer,) + zeros,
                        pipeline_mode=pl.Buffered(1))


def _params(*sem):
    return pltpu.CompilerParams(dimension_semantics=sem, vmem_limit_bytes=VMEM_LIMIT)


BF16_SUBLANES = 16


class _Cast:
    def __init__(self, w, layer, steps):
        rows, cols = w.shape[1:]
        per = -(-rows // steps)
        per = -(-per // BF16_SUBLANES) * BF16_SUBLANES
        while rows % per:
            per += BF16_SUBLANES
        last = rows // per - 1
        self.in_spec = pl.BlockSpec((None, per, cols), lambda i: (layer, jnp.minimum(i, last), 0))
        self.out_spec = pl.BlockSpec((per, cols), lambda i: (jnp.minimum(i, last), 0))
        self.out_shape = jax.ShapeDtypeStruct((rows, cols), BF16)


def _run_casts(srcs, dsts):
    for src, dst in zip(srcs, dsts):
        dst[...] = src[...].astype(BF16)


def _in_ret_kernel(*refs, tiles_per_seq, n_cast):
    (x_ref, g_ref, w_ref, cos_ref, sin_ref, dmask_ref, kdec_ref, qdec_ref, gch_ref,
     gn_ref) = refs[:10]
    cast_srcs = refs[10:10 + n_cast]
    u_ref, r_ref = refs[10 + n_cast:12 + n_cast]
    cast_dsts = refs[12 + n_cast:12 + 2 * n_cast]
    state_ref = refs[12 + 2 * n_cast]
    step = pl.program_id(0)

    @pl.when(step % tiles_per_seq == 0)
    def _():
        state_ref[...] = jnp.zeros_like(state_ref)

    _run_casts(cast_srcs, cast_dsts)
    n = _rms(x_ref[...], g_ref[...]).astype(BF16)
    proj = lambda c: _dot(n, w_ref[:, c * COL_GROUP:(c + 1) * COL_GROUP])
    n_attn = 3
    rq, rk, rv, rg = (proj(n_attn + j) for j in range(4))

    lane = lax.broadcasted_iota(jnp.int32, (RET_CHUNK, LANES), 1)
    even = (lane & 1) == 0

    def rot(x, cos, sin):
        partner = jnp.where(even, pltpu.roll(x, LANES - 1, 1), pltpu.roll(x, 1, 1))
        return x * cos + partner * sin

    heads = range(RET_HEADS)
    col = lambda h: slice(h * RET_HEAD_DIM, (h + 1) * RET_HEAD_DIM)
    state = [state_ref[h] for h in heads]
    n_chunks = IN_ROW_TILE // RET_CHUNK
    for c in range(max(n_chunks, n_attn)):
        if c < n_attn:
            u_ref[:, c * COL_GROUP:(c + 1) * COL_GROUP] = proj(c).astype(BF16)
        if c >= n_chunks:
            continue
        rows = slice(c * RET_CHUNK, (c + 1) * RET_CHUNK)
        cos, sin = cos_ref[rows, :], sin_ref[rows, :]
        q = [rot(rq[rows, col(h)], cos, sin) for h in heads]
        k = [rot(rk[rows, col(h)], cos, sin) * (RET_HEAD_DIM ** -0.5) for h in heads]
        v = [rv[rows, col(h)].astype(BF16) for h in heads]
        scores = [_dot_nt(q[h].astype(BF16), k[h].astype(BF16)) for h in heads]
        cross = [_dot((q[h] * qdec_ref[h]).astype(BF16), state[h].astype(BF16)) for h in heads]
        k_dec = [(k[h] * kdec_ref[h]).T.astype(BF16) for h in heads]
        state = [state[h] * gch_ref[h, 0:1, :] + _dot(k_dec[h], v[h]) for h in heads]
        y = [_dot((scores[h] * dmask_ref[h]).astype(BF16), v[h]) + cross[h] for h in heads]
        for h in heads:
            gate = rg[rows, col(h)]
            r_ref[rows, col(h)] = (gate * _sigmoid(gate) * _rms(y[h], gn_ref[h])).astype(BF16)
    for h in heads:
        state_ref[h] = state[h]


def _in_proj_retention(h, g, w, ret_norm_g, layer, seq_len, to_bf16):
    t = h.shape[0]
    steps = t // IN_ROW_TILE
    casts = [_Cast(w_f32, layer, steps) for w_f32 in to_bf16]
    c = RET_CHUNK
    nh = RET_HEADS
    assert ATTN_WIDTH == RET_WIDTH == COL_GROUP and seq_len % IN_ROW_TILE == 0 and IN_ROW_TILE % c == 0
    tiles_per_seq = seq_len // IN_ROW_TILE

    inv = 1.0 / (ROPE_BASE ** jnp.linspace(0.0, 1.0, RET_HEAD_DIM // 2, dtype=F32))
    ang = jnp.arange(seq_len)[:, None].astype(F32) * inv[None, :]
    cos_t = jnp.repeat(jnp.cos(ang), 2, axis=-1)
    sin_t = jnp.repeat(jnp.sin(ang), 2, axis=-1) * jnp.tile(jnp.array([-1.0, 1.0], F32), RET_HEAD_DIM // 2)
    log_g = jnp.log1p(-jnp.exp2(-5.0 - jnp.arange(nh, dtype=F32)))
    i = jnp.arange(c, dtype=F32)
    rel = i[:, None] - i[None, :]
    dmask = jnp.where(rel[None] >= 0, jnp.exp(jnp.maximum(rel, 0.0)[None] * log_g[:, None, None]), 0.0)
    k_dec = jnp.exp((c - 1 - i)[None, :] * log_g[:, None])
    q_dec = jnp.exp((i + 1.0)[None, :] * log_g[:, None])
    g_chunk = jnp.exp(c * log_g)
    k_dec = jnp.broadcast_to(k_dec[:, :, None], (nh, c, LANES))
    q_dec = jnp.broadcast_to(q_dec[:, :, None], (nh, c, LANES))
    g_chunk = jnp.broadcast_to(g_chunk[:, None, None], (nh, 8, LANES))

    row = lambda width: pl.BlockSpec((IN_ROW_TILE, width), lambda i: (i, 0))
    pos = pl.BlockSpec((IN_ROW_TILE, LANES), lambda i: (i % tiles_per_seq, 0))
    u, r, *converted = pl.pallas_call(
        functools.partial(_in_ret_kernel, tiles_per_seq=tiles_per_seq, n_cast=len(casts)),
        grid=(steps,),
        in_specs=[
            row(D_MODEL), _resident((1, D_MODEL), layer), _resident((D_MODEL, IN_WIDTH)),
            pos, pos,
            _resident((nh, c, c)), _resident((nh, c, LANES)), _resident((nh, c, LANES)),
            _resident((nh, 8, LANES)), _resident((nh, 1, RET_HEAD_DIM), layer),
        ] + [cast.in_spec for cast in casts],
        out_specs=[row(3 * ATTN_WIDTH), row(RET_WIDTH)] + [cast.out_spec for cast in casts],
        out_shape=[jax.ShapeDtypeStruct((t, 3 * ATTN_WIDTH), BF16),
                   jax.ShapeDtypeStruct((t, RET_WIDTH), BF16)] + [cast.out_shape for cast in casts],
        scratch_shapes=[pltpu.VMEM((nh, RET_HEAD_DIM, RET_HEAD_DIM), F32)],
        compiler_params=_params("arbitrary"),
        name="in_proj_retention",
    )(h, g, w, cos_t, sin_t, dmask, k_dec, q_dec, g_chunk, ret_norm_g, *to_bf16)
    return u, r, converted


ONES_ROWS = 16
VT_ROWS = ATTN_HEAD_DIM + ONES_ROWS
Q_BLOCKS = 8
Q_TILE = Q_BLOCKS * MOBA_BLOCK
PIPE_COLS = 2 * MOBA_BLOCK

def _moba_select(gate, q_block):
    nb = gate.shape[0]
    row = lax.broadcasted_iota(jnp.int32, gate.shape, 0)
    past = row < q_block
    g = jnp.where(past, gate, -jnp.inf)
    chosen = row == q_block
    for _ in range(MOBA_TOPK):
        top = jnp.max(g, axis=0, keepdims=True)
        hit = row == jnp.min(jnp.where(g == top, row, nb), axis=0, keepdims=True)
        chosen = chosen | (hit & past)
        g = jnp.where(hit, -jnp.inf, g)
    return jnp.where(chosen, 0.0, MASKED)


def _moba_kernel(q_ref, k_ref, v_ref, o_ref, kaug0_ref, kaug1_ref, vt0_ref, vt1_ref, kmean_ref,
                 sa_ref, sb_ref, *, nb):
    t = pl.program_id(2)
    blk = MOBA_BLOCK
    hd = ATTN_HEAD_DIM

    @pl.when(t == 0)
    def _():
        lane = lax.broadcasted_iota(jnp.int32, (blk, LANES), 1)
        head0_lane = lane < hd
        lane1 = lax.broadcasted_iota(jnp.int32, (1, LANES), 1)
        ones = jnp.ones((ONES_ROWS, blk), BF16)
        for n in range(nb):
            rows = slice(n * blk, (n + 1) * blk)
            kb = k_ref[0, rows, :]
            mean = jnp.sum(kb.astype(F32), axis=0, keepdims=True) * (1.0 / blk)
            kmean_ref[n:n + 1, :] = jnp.where(lane1 < hd, mean, 0.0)
            kmean_ref[nb + n:nb + n + 1, :] = jnp.where(lane1 < hd, 0.0, mean)
            kaug0_ref[n] = jnp.where(head0_lane, kb, (lane == hd + n).astype(BF16))
            kaug1_ref[n] = jnp.where(head0_lane, (lane == n).astype(BF16), kb)
            vt = v_ref[0, rows, :].astype(F32).T.astype(BF16)
            vt0_ref[n] = jnp.concatenate([vt[:hd], ones], axis=0)
            vt1_ref[n] = jnp.concatenate([vt[hd:], ones], axis=0)

    qt = (q_ref[0].astype(F32) * (hd ** -0.5 * LOG2_E)).T.astype(BF16)
    km = kmean_ref[...]
    km_hi = km.astype(BF16)
    km_lo = (km - km_hi.astype(F32)).astype(BF16)
    gate = _dot(km_hi, qt) + _dot(km_lo, qt)
    first = Q_BLOCKS * t
    zeros = lambda r: jnp.zeros((r, Q_TILE), BF16)
    kaug_refs = (kaug0_ref, kaug1_ref)
    vt_refs = (vt0_ref, vt1_ref)

    kpos = lax.broadcasted_iota(jnp.int32, (blk, blk), 0)
    qpos = lax.broadcasted_iota(jnp.int32, (blk, blk), 1)
    causal = kpos <= qpos
    group = lambda j: slice(j * blk, (j + 1) * blk)

    def softmax_pv(s, m_ref, vts):
        p = jnp.exp2(s - m_ref).astype(BF16)
        w = s.shape[1] // len(vts)
        parts = [_dot(vt, p[:, j * w:(j + 1) * w]) for j, vt in enumerate(vts)]
        return parts[0] if len(parts) == 1 else jnp.concatenate(parts, axis=1)

    qt_own = (jnp.concatenate([qt[:hd], zeros(LANES - hd)], axis=0),
              jnp.concatenate([zeros(hd), qt[hd:]], axis=0))
    for head in range(2):
        for j in range(Q_BLOCKS):
            logits = _dot(kaug_refs[head][first + j], qt_own[head][:, group(j)])
            sa_ref[head, :, group(j)] = jnp.where(causal, logits, MASKED)

    lane = lax.broadcasted_iota(jnp.int32, (1, Q_TILE), 1)
    q_block = first + sum((lane >= j * blk).astype(jnp.int32) for j in range(1, Q_BLOCKS))
    mask0 = _moba_select(gate[:nb], q_block).astype(BF16)
    mask1 = _moba_select(gate[nb:], q_block).astype(BF16)
    qt_sel = (jnp.concatenate([qt[:hd], mask0, zeros(LANES - hd - nb)], axis=0),
              jnp.concatenate([mask1, zeros(hd - nb), qt[hd:]], axis=0))

    def chunks(skip):
        width = Q_TILE - skip * blk
        return [(a, min(a + PIPE_COLS, width)) for a in range(0, width, PIPE_COLS)]

    def issue_chunk(s_ref, n, head, skip, a, b_):
        q0 = skip * blk
        s_ref[head, :, a:b_] = _dot(kaug_refs[head][n], qt_sel[head][:, q0 + a:q0 + b_])

    def issue_head(s_ref, n, head, skip=0):
        for a, b_ in chunks(skip):
            issue_chunk(s_ref, n, head, skip, a, b_)

    def consume_chunk(s_ref, n, head, m, acc, skip, a, b_):
        q0 = skip * blk
        s = s_ref[head, :, a:b_]
        m_old = m[:, q0 + a:q0 + b_]
        m_new = jnp.maximum(m_old, jnp.max(s, axis=0, keepdims=True))
        return m_new, (acc[:, q0 + a:q0 + b_] * jnp.exp2(m_old - m_new)
                       + softmax_pv(s, m_new, [vt_refs[head][n]]))

    def step(dst_ref, n_issue, src_ref, n_consume, carry, issue_skip=0, consume_skip=0):
        out = ()
        for head in range(2):
            m, acc = carry[2 * head], carry[2 * head + 1]
            q0 = consume_skip * blk
            m_cols = [m[:, :q0]] if q0 else []
            acc_cols = [acc[:, :q0]] if q0 else []
            todo_issue, todo_consume = chunks(issue_skip), chunks(consume_skip)
            for k in range(max(len(todo_issue), len(todo_consume))):
                if k < len(todo_issue):
                    issue_chunk(dst_ref, n_issue, head, issue_skip, *todo_issue[k])
                if k < len(todo_consume):
                    m_new, acc_new = consume_chunk(src_ref, n_consume, head, m, acc, consume_skip,
                                                   *todo_consume[k])
                    m_cols.append(m_new)
                    acc_cols.append(acc_new)
            out += (jnp.concatenate(m_cols, axis=1), jnp.concatenate(acc_cols, axis=1))
        return out

    bufs = (sb_ref, sa_ref)
    for head in range(2):
        issue_head(sb_ref, first, head, skip=1)
    carry = ()
    for head in range(2):
        s = sa_ref[head]
        m = jnp.max(s, axis=0, keepdims=True)
        carry += (m, softmax_pv(s, m, [vt_refs[head][first + j] for j in range(Q_BLOCKS)]))
    for j in range(Q_BLOCKS - 1):
        if j + 1 < Q_BLOCKS - 1:
            carry = step(bufs[(j + 1) % 2], first + j + 1, bufs[j % 2], first + j, carry,
                         issue_skip=j + 2, consume_skip=j + 1)
        else:
            carry = step(bufs[(j + 1) % 2], 0, bufs[j % 2], first + j, carry, consume_skip=j + 1)

    def two_blocks(n, carry):
        carry = step(sb_ref, n + 1, sa_ref, n, carry)
        return step(sa_ref, n + 2, sb_ref, n + 1, carry)

    carry = lax.fori_loop(
        0, first // 4, lambda j, c: two_blocks(4 * j + 2, two_blocks(4 * j, c)), carry)
    if Q_BLOCKS % 4:
        carry = lax.fori_loop(0, (first % 4) // 2, lambda _, c: two_blocks(first - 2, c), carry)
    _, a0, _, a1 = carry
    out_t = jnp.concatenate([a0[:hd] / a0[hd:hd + 1], a1[:hd] / a1[hd:hd + 1]], axis=0)
    o_ref[0] = out_t.T.astype(BF16)


def _moba(u):
    b, s, _ = u.shape
    nb = s // MOBA_BLOCK
    assert nb % ONES_ROWS == 0 and nb <= ATTN_HEAD_DIM and s % Q_TILE == 0 and Q_BLOCKS % 2 == 0
    pairs = ATTN_WIDTH // LANES
    return pl.pallas_call(
        functools.partial(_moba_kernel, nb=nb),
        grid=(b, pairs, s // Q_TILE),
        in_specs=[
            pl.BlockSpec((1, Q_TILE, LANES), lambda bi, hp, t: (bi, t, hp)),
            pl.BlockSpec((1, s, LANES), lambda bi, hp, t: (bi, 0, pairs + hp)),
            pl.BlockSpec((1, s, LANES), lambda bi, hp, t: (bi, 0, 2 * pairs + hp)),
        ],
        out_specs=pl.BlockSpec((1, Q_TILE, LANES), lambda bi, hp, t: (bi, t, hp)),
        out_shape=jax.ShapeDtypeStruct((b, s, ATTN_WIDTH), BF16),
        scratch_shapes=[
            pltpu.VMEM((nb, MOBA_BLOCK, LANES), BF16),
            pltpu.VMEM((nb, MOBA_BLOCK, LANES), BF16),
            pltpu.VMEM((nb, VT_ROWS, MOBA_BLOCK), BF16),
            pltpu.VMEM((nb, VT_ROWS, MOBA_BLOCK), BF16),
            pltpu.VMEM((2 * nb, LANES), F32),
            pltpu.VMEM((2, MOBA_BLOCK, Q_TILE), F32),
            pltpu.VMEM((2, MOBA_BLOCK, Q_TILE), F32),
        ],
        compiler_params=_params("parallel", "parallel", "arbitrary"),
        name="moba",
    )(u, u, u)


def _tail_kernel(*refs, final_norm, n_cast):
    (h_ref, a_ref, r_ref, p_ref, wo_ref, g_ffn_ref, w_in_ref, w_out_ref, g_ple_ref, wpg_ref,
     wpp_ref, g_final_ref) = refs[:12]
    cast_srcs = refs[12:12 + n_cast]
    o_ref = refs[12 + n_cast]
    cast_dsts = refs[13 + n_cast:13 + 2 * n_cast]
    act_ref = refs[13 + 2 * n_cast]
    _run_casts(cast_srcs, cast_dsts)
    h1 = (h_ref[...] + _dot(a_ref[...], wo_ref[:ATTN_WIDTH, :])
          + _dot(r_ref[...], wo_ref[ATTN_WIDTH:, :]))
    n1 = _rms(h1, g_ffn_ref[...]).astype(BF16)
    proj = _dot(p_ref[...].astype(BF16), wpp_ref[...])
    for c in range(N_FF_CHUNKS):
        zg = _dot(n1, w_in_ref[:, c * FF_CHUNK:(c + 1) * FF_CHUNK])
        zu = _dot(n1, w_in_ref[:, D_FF + c * FF_CHUNK:D_FF + (c + 1) * FF_CHUNK])
        act_ref[:, c * FF_CHUNK:(c + 1) * FF_CHUNK] = (zg * _sigmoid(zg) * zu).astype(BF16)
    h2 = h1 + _dot(act_ref[...], w_out_ref[...])
    gate = _sigmoid(_dot(_rms(h2, g_ple_ref[...]).astype(BF16), wpg_ref[...]))
    out = h2 + gate * proj
    if final_norm:
        out = _rms(out, g_final_ref[...])
    o_ref[...] = out


def _tail(h, a, r, p, layer, wo, g_ffn, w_ffn_in, w_ffn_out, g_ple, wpg, wpp, g_final, final_norm,
          to_bf16):
    t = h.shape[0]
    steps = t // ROW_TILE
    casts = [_Cast(w_f32, cast_layer, steps) for w_f32, cast_layer in to_bf16]
    row = lambda w: pl.BlockSpec((ROW_TILE, w), lambda i: (i, 0))
    out, *converted = pl.pallas_call(
        functools.partial(_tail_kernel, final_norm=final_norm, n_cast=len(casts)),
        grid=(steps,),
        in_specs=[row(D_MODEL), row(ATTN_WIDTH), row(RET_WIDTH),
                  pl.BlockSpec((ROW_TILE, PLE_DIM), lambda i: (layer * steps + i, 0)),
                  _resident((ATTN_WIDTH + RET_WIDTH, D_MODEL)),
                  _resident((1, D_MODEL), layer), _resident((D_MODEL, 2 * D_FF)),
                  _resident((D_FF, D_MODEL)), _resident((1, D_MODEL), layer),
                  _resident((D_MODEL, D_MODEL)), _resident((PLE_DIM, D_MODEL)),
                  _resident((1, D_MODEL))] + [cast.in_spec for cast in casts],
        out_specs=[row(D_MODEL)] + [cast.out_spec for cast in casts],
        out_shape=[jax.ShapeDtypeStruct((t, D_MODEL), F32)] + [cast.out_shape for cast in casts],
        scratch_shapes=[pltpu.VMEM((ROW_TILE, D_FF), BF16)],
        compiler_params=_params("arbitrary"),
        name="tail",
    )(h, a, r, p, wo, g_ffn, w_ffn_in, w_ffn_out, g_ple, wpg, wpp, g_final,
      *[w_f32 for w_f32, _ in to_bf16])
    return out, converted


def kernel(x, p, attn_norm_g, w_in, ret_norm_g, w_out, ffn_norm_g, w_ffn_in, w_ffn_out,
           ple_norm_g, w_ple_gate, w_ple_proj, final_norm_g):
    b, s, d = x.shape
    depth = p.shape[0]
    assert d == D_MODEL and s % MOBA_BLOCK == 0 and s >= (MOBA_TOPK + 1) * MOBA_BLOCK
    assert (b * s) % ROW_TILE == 0 and s // MOBA_BLOCK <= LANES - ATTN_HEAD_DIM
    t = b * s
    h = x.reshape(t, d)
    gains = lambda g: g.reshape(depth, 1, -1).astype(F32)
    attn_g, ffn_g, ple_g = gains(attn_norm_g), gains(ffn_norm_g), gains(ple_norm_g)
    ret_g = ret_norm_g.reshape(depth, RET_HEADS, 1, RET_HEAD_DIM).astype(F32)
    p_rows = p.reshape(depth * t, PLE_DIM)
    tail_weights = [w_out, w_ffn_in, w_ffn_out, w_ple_gate, w_ple_proj]
    w_in_bf16 = w_in[0].astype(BF16)
    for i in range(depth):
        last = i == depth - 1
        u, r, tail_bf16 = _in_proj_retention(h, attn_g, w_in_bf16, ret_g, i, s, tail_weights)
        a = _moba(u.reshape(b, s, 3 * ATTN_WIDTH)).reshape(t, ATTN_WIDTH)
        wo, wf_in, wf_out, wpg, wpp = tail_bf16
        h, next_w_in = _tail(h, a, r, p_rows, i, wo, ffn_g, wf_in, wf_out, ple_g, wpg, wpp,
                             final_norm_g.reshape(1, -1).astype(F32), final_norm=last,
                             to_bf16=[] if last else [(w_in, i + 1)])
        if not last:
            (w_in_bf16,) = next_w_in
    return h.reshape(b, s, d)
```

```python
import functools

import jax
import jax.numpy as jnp
from jax import lax
from jax.experimental import pallas as pl
from jax.experimental.pallas import tpu as pltpu

D_MODEL = 1024
PLE_DIM = 256
ATTN_WIDTH = 512
ATTN_HEAD_DIM = 64
RET_WIDTH = 512
RET_HEADS = 4
RET_HEAD_DIM = 128
IN_WIDTH = 3 * ATTN_WIDTH + 4 * RET_WIDTH
MOBA_BLOCK = 256
MOBA_TOPK = 3
RET_CHUNK = 256
ROPE_BASE = 10000.0
D_FF = 2816
EPS = 1e-6

LANES = 128
FF_CHUNK = 256
N_FF_CHUNKS = D_FF // FF_CHUNK
ROW_TILE = 512
IN_ROW_TILE = 1024
COL_GROUP = 512
MASKED = -1e30
LOG2_E = 1.4426950408889634
VMEM_LIMIT = 52 * 1024 * 1024

F32 = jnp.float32
BF16 = jnp.bfloat16


def _dot(a, b):
    return jnp.dot(a, b, preferred_element_type=F32)


def _dot_nt(a, b):
    return lax.dot_general(a, b, (((1,), (1,)), ((), ())), preferred_element_type=F32)


def _rms(x, g):
    return x * lax.rsqrt(jnp.mean(x * x, axis=-1, keepdims=True) + EPS) * g


def _sigmoid(x):
    return 1.0 / (1.0 + jnp.exp(-x))


def _resident(shape, layer=None):
    zeros = (0,) * len(shape)
    if layer is None:
        return pl.BlockSpec(shape, lambda *_: zeros, pipeline_mode=pl.Buffered(1))
    return pl.BlockSpec((None,) + tuple(shape), lambda *_: (layer,) + zeros,
                        pipeline_mode=pl.Buffered(1))


def _params(*sem):
    return pltpu.CompilerParams(dimension_semantics=sem, vmem_limit_bytes=VMEM_LIMIT)


BF16_SUBLANES = 16


class _Cast:
    def __init__(self, w, layer, steps):
        rows, cols = w.shape[1:]
        per = -(-rows // steps)
        per = -(-per // BF16_SUBLANES) * BF16_SUBLANES
        while rows % per:
            per += BF16_SUBLANES
        last = rows // per - 1
        self.in_spec = pl.BlockSpec((None, per, cols), lambda i: (layer, jnp.minimum(i, last), 0))
        self.out_spec = pl.BlockSpec((per, cols), lambda i: (jnp.minimum(i, last), 0))
        self.out_shape = jax.ShapeDtypeStruct((rows, cols), BF16)


def _run_casts(srcs, dsts):
    for src, dst in zip(srcs, dsts):
        dst[...] = src[...].astype(BF16)


def _in_ret_kernel(*refs, tiles_per_seq, n_cast):
    (x_ref, g_ref, w_ref, cos_ref, sin_ref, dmask_ref, kdec_ref, qdec_ref, gch_ref,
     gn_ref) = refs[:10]
    cast_srcs = refs[10:10 + n_cast]
    u_ref, r_ref = refs[10 + n_cast:12 + n_cast]
    cast_dsts = refs[12 + n_cast:12 + 2 * n_cast]
    state_ref = refs[12 + 2 * n_cast]
    step = pl.program_id(0)

    @pl.when(step % tiles_per_seq == 0)
    def _():
        state_ref[...] = jnp.zeros_like(state_ref)

    _run_casts(cast_srcs, cast_dsts)
    n = _rms(x_ref[...], g_ref[...]).astype(BF16)
    proj = lambda c: _dot(n, w_ref[:, c * COL_GROUP:(c + 1) * COL_GROUP])
    n_attn = 3
    rq, rk, rv, rg = (proj(n_attn + j) for j in range(4))

    lane = lax.broadcasted_iota(jnp.int32, (RET_CHUNK, LANES), 1)
    even = (lane & 1) == 0

    def rot(x, cos, sin):
        partner = jnp.where(even, pltpu.roll(x, LANES - 1, 1), pltpu.roll(x, 1, 1))
        return x * cos + partner * sin

    heads = range(RET_HEADS)
    col = lambda h: slice(h * RET_HEAD_DIM, (h + 1) * RET_HEAD_DIM)
    state = [state_ref[h] for h in heads]
    n_chunks = IN_ROW_TILE // RET_CHUNK
    for c in range(max(n_chunks, n_attn)):
        if c < n_attn:
            u_ref[:, c * COL_GROUP:(c + 1) * COL_GROUP] = proj(c).astype(BF16)
        if c >= n_chunks:
            continue
        rows = slice(c * RET_CHUNK, (c + 1) * RET_CHUNK)
        cos, sin = cos_ref[rows, :], sin_ref[rows, :]
        q = [rot(rq[rows, col(h)], cos, sin) for h in heads]
        k = [rot(rk[rows, col(h)], cos, sin) * (RET_HEAD_DIM ** -0.5) for h in heads]
        v = [rv[rows, col(h)].astype(BF16) for h in heads]
        scores = [_dot_nt(q[h].astype(BF16), k[h].astype(BF16)) for h in heads]
        cross = [_dot((q[h] * qdec_ref[h]).astype(BF16), state[h].astype(BF16)) for h in heads]
        k_dec = [(k[h] * kdec_ref[h]).T.astype(BF16) for h in heads]
        state = [state[h] * gch_ref[h, 0:1, :] + _dot(k_dec[h], v[h]) for h in heads]
        y = [_dot((scores[h] * dmask_ref[h]).astype(BF16), v[h]) + cross[h] for h in heads]
        for h in heads:
            gate = rg[rows, col(h)]
            r_ref[rows, col(h)] = (gate * _sigmoid(gate) * _rms(y[h], gn_ref[h])).astype(BF16)
    for h in heads:
        state_ref[h] = state[h]


def _in_proj_retention(h, g, w, ret_norm_g, layer, seq_len, to_bf16):
    t = h.shape[0]
    steps = t // IN_ROW_TILE
    casts = [_Cast(w_f32, layer, steps) for w_f32 in to_bf16]
    c = RET_CHUNK
    nh = RET_HEADS
    assert ATTN_WIDTH == RET_WIDTH == COL_GROUP and seq_len % IN_ROW_TILE == 0 and IN_ROW_TILE % c == 0
    tiles_per_seq = seq_len // IN_ROW_TILE

    inv = 1.0 / (ROPE_BASE ** jnp.linspace(0.0, 1.0, RET_HEAD_DIM // 2, dtype=F32))
    ang = jnp.arange(seq_len)[:, None].astype(F32) * inv[None, :]
    cos_t = jnp.repeat(jnp.cos(ang), 2, axis=-1)
    sin_t = jnp.repeat(jnp.sin(ang), 2, axis=-1) * jnp.tile(jnp.array([-1.0, 1.0], F32), RET_HEAD_DIM // 2)
    log_g = jnp.log1p(-jnp.exp2(-5.0 - jnp.arange(nh, dtype=F32)))
    i = jnp.arange(c, dtype=F32)
    rel = i[:, None] - i[None, :]
    dmask = jnp.where(rel[None] >= 0, jnp.exp(jnp.maximum(rel, 0.0)[None] * log_g[:, None, None]), 0.0)
    k_dec = jnp.exp((c - 1 - i)[None, :] * log_g[:, None])
    q_dec = jnp.exp((i + 1.0)[None, :] * log_g[:, None])
    g_chunk = jnp.exp(c * log_g)
    k_dec = jnp.broadcast_to(k_dec[:, :, None], (nh, c, LANES))
    q_dec = jnp.broadcast_to(q_dec[:, :, None], (nh, c, LANES))
    g_chunk = jnp.broadcast_to(g_chunk[:, None, None], (nh, 8, LANES))

    row = lambda width: pl.BlockSpec((IN_ROW_TILE, width), lambda i: (i, 0))
    pos = pl.BlockSpec((IN_ROW_TILE, LANES), lambda i: (i % tiles_per_seq, 0))
    u, r, *converted = pl.pallas_call(
        functools.partial(_in_ret_kernel, tiles_per_seq=tiles_per_seq, n_cast=len(casts)),
        grid=(steps,),
        in_specs=[
            row(D_MODEL), _resident((1, D_MODEL), layer), _resident((D_MODEL, IN_WIDTH)),
            pos, pos,
            _resident((nh, c, c)), _resident((nh, c, LANES)), _resident((nh, c, LANES)),
            _resident((nh, 8, LANES)), _resident((nh, 1, RET_HEAD_DIM), layer),
        ] + [cast.in_spec for cast in casts],
        out_specs=[row(3 * ATTN_WIDTH), row(RET_WIDTH)] + [cast.out_spec for cast in casts],
        out_shape=[jax.ShapeDtypeStruct((t, 3 * ATTN_WIDTH), BF16),
                   jax.ShapeDtypeStruct((t, RET_WIDTH), BF16)] + [cast.out_shape for cast in casts],
        scratch_shapes=[pltpu.VMEM((nh, RET_HEAD_DIM, RET_HEAD_DIM), F32)],
        compiler_params=_params("arbitrary"),
        name="in_proj_retention",
    )(h, g, w, cos_t, sin_t, dmask, k_dec, q_dec, g_chunk, ret_norm_g, *to_bf16)
    return u, r, converted


ONES_ROWS = 16
VT_ROWS = ATTN_HEAD_DIM + ONES_ROWS
Q_BLOCKS = 16
Q_TILE = Q_BLOCKS * MOBA_BLOCK
PIPE_COLS = 2 * MOBA_BLOCK

def _moba_select(gate, q_block):
    nb = gate.shape[0]
    row = lax.broadcasted_iota(jnp.int32, gate.shape, 0)
    past = row < q_block
    g = jnp.where(past, gate, -jnp.inf)
    chosen = row == q_block
    for _ in range(MOBA_TOPK):
        top = jnp.max(g, axis=0, keepdims=True)
        hit = row == jnp.min(jnp.where(g == top, row, nb), axis=0, keepdims=True)
        chosen = chosen | (hit & past)
        g = jnp.where(hit, -jnp.inf, g)
    return jnp.where(chosen, 0.0, MASKED)


def _moba_kernel(q_ref, k_ref, v_ref, o_ref, kaug0_ref, kaug1_ref, vt0_ref, vt1_ref, kmean_ref,
                 sa_ref, sb_ref, *, nb):
    t = pl.program_id(2)
    blk = MOBA_BLOCK
    hd = ATTN_HEAD_DIM

    @pl.when(t == 0)
    def _():
        lane = lax.broadcasted_iota(jnp.int32, (blk, LANES), 1)
        head0_lane = lane < hd
        lane1 = lax.broadcasted_iota(jnp.int32, (1, LANES), 1)
        ones = jnp.ones((ONES_ROWS, blk), BF16)
        for n in range(nb):
            rows = slice(n * blk, (n + 1) * blk)
            kb = k_ref[0, rows, :]
            mean = jnp.sum(kb.astype(F32), axis=0, keepdims=True) * (1.0 / blk)
            kmean_ref[n:n + 1, :] = jnp.where(lane1 < hd, mean, 0.0)
            kmean_ref[nb + n:nb + n + 1, :] = jnp.where(lane1 < hd, 0.0, mean)
            kaug0_ref[n] = jnp.where(head0_lane, kb, (lane == hd + n).astype(BF16))
            kaug1_ref[n] = jnp.where(head0_lane, (lane == n).astype(BF16), kb)
            vt = v_ref[0, rows, :].astype(F32).T.astype(BF16)
            vt0_ref[n] = jnp.concatenate([vt[:hd], ones], axis=0)
            vt1_ref[n] = jnp.concatenate([vt[hd:], ones], axis=0)

    qt = (q_ref[0].astype(F32) * (hd ** -0.5 * LOG2_E)).T.astype(BF16)
    km = kmean_ref[...]
    km_hi = km.astype(BF16)
    km_lo = (km - km_hi.astype(F32)).astype(BF16)
    gate = _dot(km_hi, qt) + _dot(km_lo, qt)
    first = Q_BLOCKS * t
    zeros = lambda r: jnp.zeros((r, Q_TILE), BF16)
    kaug_refs = (kaug0_ref, kaug1_ref)
    vt_refs = (vt0_ref, vt1_ref)

    kpos = lax.broadcasted_iota(jnp.int32, (blk, blk), 0)
    qpos = lax.broadcasted_iota(jnp.int32, (blk, blk), 1)
    causal = kpos <= qpos
    group = lambda j: slice(j * blk, (j + 1) * blk)

    def softmax_pv(s, m_ref, vts):
        p = jnp.exp2(s - m_ref).astype(BF16)
        w = s.shape[1] // len(vts)
        parts = [_dot(vt, p[:, j * w:(j + 1) * w]) for j, vt in enumerate(vts)]
        return parts[0] if len(parts) == 1 else jnp.concatenate(parts, axis=1)

    qt_own = (jnp.concatenate([qt[:hd], zeros(LANES - hd)], axis=0),
              jnp.concatenate([zeros(hd), qt[hd:]], axis=0))
    for head in range(2):
        for j in range(Q_BLOCKS):
            logits = _dot(kaug_refs[head][first + j], qt_own[head][:, group(j)])
            sa_ref[head, :, group(j)] = jnp.where(causal, logits, MASKED)

    lane = lax.broadcasted_iota(jnp.int32, (1, Q_TILE), 1)
    q_block = first + sum((lane >= j * blk).astype(jnp.int32) for j in range(1, Q_BLOCKS))
    mask0 = _moba_select(gate[:nb], q_block).astype(BF16)
    mask1 = _moba_select(gate[nb:], q_block).astype(BF16)
    qt_sel = (jnp.concatenate([qt[:hd], mask0, zeros(LANES - hd - nb)], axis=0),
              jnp.concatenate([mask1, zeros(hd - nb), qt[hd:]], axis=0))

    def chunks(skip):
        width = Q_TILE - skip * blk
        return [(a, min(a + PIPE_COLS, width)) for a in range(0, width, PIPE_COLS)]

    def issue_chunk(s_ref, n, head, skip, a, b_):
        q0 = skip * blk
        s_ref[head, :, a:b_] = _dot(kaug_refs[head][n], qt_sel[head][:, q0 + a:q0 + b_])

    def issue_head(s_ref, n, head, skip=0):
        for a, b_ in chunks(skip):
            issue_chunk(s_ref, n, head, skip, a, b_)

    def consume_chunk(s_ref, n, head, m, acc, skip, a, b_):
        q0 = skip * blk
        s = s_ref[head, :, a:b_]
        m_old = m[:, q0 + a:q0 + b_]
        m_new = jnp.maximum(m_old, jnp.max(s, axis=0, keepdims=True))
        return m_new, (acc[:, q0 + a:q0 + b_] * jnp.exp2(m_old - m_new)
                       + softmax_pv(s, m_new, [vt_refs[head][n]]))

    def step(dst_ref, n_issue, src_ref, n_consume, carry, issue_skip=0, consume_skip=0):
        out = ()
        for head in range(2):
            m, acc = carry[2 * head], carry[2 * head + 1]
            q0 = consume_skip * blk
            m_cols = [m[:, :q0]] if q0 else []
            acc_cols = [acc[:, :q0]] if q0 else []
            todo_issue, todo_consume = chunks(issue_skip), chunks(consume_skip)
            for k in range(max(len(todo_issue), len(todo_consume))):
                if k < len(todo_issue):
                    issue_chunk(dst_ref, n_issue, head, issue_skip, *todo_issue[k])
                if k < len(todo_consume):
                    m_new, acc_new = consume_chunk(src_ref, n_consume, head, m, acc, consume_skip,
                                                   *todo_consume[k])
                    m_cols.append(m_new)
                    acc_cols.append(acc_new)
            out += (jnp.concatenate(m_cols, axis=1), jnp.concatenate(acc_cols, axis=1))
        return out

    bufs = (sb_ref, sa_ref)
    for head in range(2):
        issue_head(sb_ref, first, head, skip=1)
    carry = ()
    for head in range(2):
        s = sa_ref[head]
        m = jnp.max(s, axis=0, keepdims=True)
        carry += (m, softmax_pv(s, m, [vt_refs[head][first + j] for j in range(Q_BLOCKS)]))
    for j in range(Q_BLOCKS - 1):
        if j + 1 < Q_BLOCKS - 1:
            carry = step(bufs[(j + 1) % 2], first + j + 1, bufs[j % 2], first + j, carry,
                         issue_skip=j + 2, consume_skip=j + 1)
        else:
            carry = step(bufs[(j + 1) % 2], 0, bufs[j % 2], first + j, carry, consume_skip=j + 1)

    def two_blocks(n, carry):
        carry = step(sb_ref, n + 1, sa_ref, n, carry)
        return step(sa_ref, n + 2, sb_ref, n + 1, carry)

    carry = lax.fori_loop(
        0, first // 4, lambda j, c: two_blocks(4 * j + 2, two_blocks(4 * j, c)), carry)
    if Q_BLOCKS % 4:
        carry = lax.fori_loop(0, (first % 4) // 2, lambda _, c: two_blocks(first - 2, c), carry)
    _, a0, _, a1 = carry
    out_t = jnp.concatenate([a0[:hd] / a0[hd:hd + 1], a1[:hd] / a1[hd:hd + 1]], axis=0)
    o_ref[0] = out_t.T.astype(BF16)


def _moba(u):
    b, s, _ = u.shape
    nb = s // MOBA_BLOCK
    assert nb % ONES_ROWS == 0 and nb <= ATTN_HEAD_DIM and s % Q_TILE == 0 and Q_BLOCKS % 2 == 0
    pairs = ATTN_WIDTH // LANES
    return pl.pallas_call(
        functools.partial(_moba_kernel, nb=nb),
        grid=(b, pairs, s // Q_TILE),
        in_specs=[
            pl.BlockSpec((1, Q_TILE, LANES), lambda bi, hp, t: (bi, t, hp)),
            pl.BlockSpec((1, s, LANES), lambda bi, hp, t: (bi, 0, pairs + hp)),
            pl.BlockSpec((1, s, LANES), lambda bi, hp, t: (bi, 0, 2 * pairs + hp)),
        ],
        out_specs=pl.BlockSpec((1, Q_TILE, LANES), lambda bi, hp, t: (bi, t, hp)),
        out_shape=jax.ShapeDtypeStruct((b, s, ATTN_WIDTH), BF16),
        scratch_shapes=[
            pltpu.VMEM((nb, MOBA_BLOCK, LANES), BF16),
            pltpu.VMEM((nb, MOBA_BLOCK, LANES), BF16),
            pltpu.VMEM((nb, VT_ROWS, MOBA_BLOCK), BF16),
            pltpu.VMEM((nb, VT_ROWS, MOBA_BLOCK), BF16),
            pltpu.VMEM((2 * nb, LANES), F32),
            pltpu.VMEM((2, MOBA_BLOCK, Q_TILE), F32),
            pltpu.VMEM((2, MOBA_BLOCK, Q_TILE), F32),
        ],
        compiler_params=_params("parallel", "parallel", "arbitrary"),
        name="moba",
    )(u, u, u)


def _tail_kernel(*refs, final_norm, n_cast):
    (h_ref, a_ref, r_ref, p_ref, wo_ref, g_ffn_ref, w_in_ref, w_out_ref, g_ple_ref, wpg_ref,
     wpp_ref, g_final_ref) = refs[:12]
    cast_srcs = refs[12:12 + n_cast]
    o_ref = refs[12 + n_cast]
    cast_dsts = refs[13 + n_cast:13 + 2 * n_cast]
    act_ref = refs[13 + 2 * n_cast]
    _run_casts(cast_srcs, cast_dsts)
    halves = [slice(i * ROW_TILE // 2, (i + 1) * ROW_TILE // 2) for i in range(2)]
    h1 = [h_ref[rows, :] + _dot(a_ref[rows, :], wo_ref[:ATTN_WIDTH, :])
          + _dot(r_ref[rows, :], wo_ref[ATTN_WIDTH:, :]) for rows in halves]
    proj = [_dot(p_ref[rows, :].astype(BF16), wpp_ref[...]) for rows in halves]
    n1 = [_rms(x, g_ffn_ref[...]).astype(BF16) for x in h1]
    for n, rows in zip(n1, halves):
        for c in range(N_FF_CHUNKS):
            zg = _dot(n, w_in_ref[:, c * FF_CHUNK:(c + 1) * FF_CHUNK])
            zu = _dot(n, w_in_ref[:, D_FF + c * FF_CHUNK:D_FF + (c + 1) * FF_CHUNK])
            act_ref[rows, c * FF_CHUNK:(c + 1) * FF_CHUNK] = (zg * _sigmoid(zg) * zu).astype(BF16)
    h2 = [x + _dot(act_ref[rows, :], w_out_ref[...]) for x, rows in zip(h1, halves)]
    gate = [_sigmoid(_dot(_rms(x, g_ple_ref[...]).astype(BF16), wpg_ref[...])) for x in h2]
    for x, g, pr, rows in zip(h2, gate, proj, halves):
        out = x + g * pr
        if final_norm:
            out = _rms(out, g_final_ref[...])
        o_ref[rows, :] = out


def _tail(h, a, r, p, layer, wo, g_ffn, w_ffn_in, w_ffn_out, g_ple, wpg, wpp, g_final, final_norm,
          to_bf16):
    t = h.shape[0]
    steps = t // ROW_TILE
    casts = [_Cast(w_f32, cast_layer, steps) for w_f32, cast_layer in to_bf16]
    row = lambda w: pl.BlockSpec((ROW_TILE, w), lambda i: (i, 0))
    out, *converted = pl.pallas_call(
        functools.partial(_tail_kernel, final_norm=final_norm, n_cast=len(casts)),
        grid=(steps,),
        in_specs=[row(D_MODEL), row(ATTN_WIDTH), row(RET_WIDTH),
                  pl.BlockSpec((ROW_TILE, PLE_DIM), lambda i: (layer * steps + i, 0)),
                  _resident((ATTN_WIDTH + RET_WIDTH, D_MODEL)),
                  _resident((1, D_MODEL), layer), _resident((D_MODEL, 2 * D_FF)),
                  _resident((D_FF, D_MODEL)), _resident((1, D_MODEL), layer),
                  _resident((D_MODEL, D_MODEL)), _resident((PLE_DIM, D_MODEL)),
                  _resident((1, D_MODEL))] + [cast.in_spec for cast in casts],
        out_specs=[row(D_MODEL)] + [cast.out_spec for cast in casts],
        out_shape=[jax.ShapeDtypeStruct((t, D_MODEL), F32)] + [cast.out_shape for cast in casts],
        scratch_shapes=[pltpu.VMEM((ROW_TILE, D_FF), BF16)],
        compiler_params=_params("arbitrary"),
        name="tail",
    )(h, a, r, p, wo, g_ffn, w_ffn_in, w_ffn_out, g_ple, wpg, wpp, g_final,
      *[w_f32 for w_f32, _ in to_bf16])
    return out, converted


def kernel(x, p, attn_norm_g, w_in, ret_norm_g, w_out, ffn_norm_g, w_ffn_in, w_ffn_out,
           ple_norm_g, w_ple_gate, w_ple_proj, final_norm_g):
    b, s, d = x.shape
    depth = p.shape[0]
    assert d == D_MODEL and s % MOBA_BLOCK == 0 and s >= (MOBA_TOPK + 1) * MOBA_BLOCK
    assert (b * s) % ROW_TILE == 0 and s // MOBA_BLOCK <= LANES - ATTN_HEAD_DIM
    t = b * s
    h = x.reshape(t, d)
    gains = lambda g: g.reshape(depth, 1, -1).astype(F32)
    attn_g, ffn_g, ple_g = gains(attn_norm_g), gains(ffn_norm_g), gains(ple_norm_g)
    ret_g = ret_norm_g.reshape(depth, RET_HEADS, 1, RET_HEAD_DIM).astype(F32)
    p_rows = p.reshape(depth * t, PLE_DIM)
    tail_weights = [w_out, w_ffn_in, w_ffn_out, w_ple_gate, w_ple_proj]
    w_in_bf16 = w_in[0].astype(BF16)
    for i in range(depth):
        last = i == depth - 1
        u, r, tail_bf16 = _in_proj_retention(h, attn_g, w_in_bf16, ret_g, i, s, tail_weights)
        a = _moba(u.reshape(b, s, 3 * ATTN_WIDTH)).reshape(t, ATTN_WIDTH)
        wo, wf_in, wf_out, wpg, wpp = tail_bf16
        h, next_w_in = _tail(h, a, r, p_rows, i, wo, ffn_g, wf_in, wf_out, ple_g, wpg, wpp,
                             final_norm_g.reshape(1, -1).astype(F32), final_norm=last,
                             to_bf16=[] if last else [(w_in, i + 1)])
        if not last:
            (w_in_bf16,) = next_w_in
    return h.reshape(b, s, d)
```

```python
import functools

import jax
import jax.numpy as jnp
from jax import lax
from jax.experimental import pallas as pl
from jax.experimental.pallas import tpu as pltpu

D_MODEL = 1024
PLE_DIM = 256
ATTN_WIDTH = 512
ATTN_HEAD_DIM = 64
RET_WIDTH = 512
RET_HEADS = 4
RET_HEAD_DIM = 128
IN_WIDTH = 3 * ATTN_WIDTH + 4 * RET_WIDTH
MOBA_BLOCK = 256
MOBA_TOPK = 3
RET_CHUNK = 256
ROPE_BASE = 10000.0
D_FF = 2816
EPS = 1e-6

LANES = 128
FF_CHUNK = 256
N_FF_CHUNKS = D_FF // FF_CHUNK
ROW_TILE = 512
IN_ROW_TILE = 1024
COL_GROUP = 512
MASKED = -1e30
LOG2_E = 1.4426950408889634
VMEM_LIMIT = 52 * 1024 * 1024

F32 = jnp.float32
BF16 = jnp.bfloat16


def _dot(a, b):
    return jnp.dot(a, b, preferred_element_type=F32)


def _dot_nt(a, b):
    return lax.dot_general(a, b, (((1,), (1,)), ((), ())), preferred_element_type=F32)


def _rms(x, g):
    return x * lax.rsqrt(jnp.mean(x * x, axis=-1, keepdims=True) + EPS) * g


def _sigmoid(x):
    return 1.0 / (1.0 + jnp.exp(-x))


def _resident(shape, layer=None):
    zeros = (0,) * len(shape)
    if layer is None:
        return pl.BlockSpec(shape, lambda *_: zeros, pipeline_mode=pl.Buffered(1))
    return pl.BlockSpec((None,) + tuple(shape), lambda *_: (layer,) + zeros,
                        pipeline_mode=pl.Buffered(1))


def _params(*sem):
    return pltpu.CompilerParams(dimension_semantics=sem, vmem_limit_bytes=VMEM_LIMIT)


BF16_SUBLANES = 16


class _Cast:
    def __init__(self, w, layer, steps):
        rows, cols = w.shape[1:]
        per = -(-rows // steps)
        per = -(-per // BF16_SUBLANES) * BF16_SUBLANES
        while rows % per:
            per += BF16_SUBLANES
        last = rows // per - 1
        self.in_spec = pl.BlockSpec((None, per, cols), lambda i: (layer, jnp.minimum(i, last), 0))
        self.out_spec = pl.BlockSpec((per, cols), lambda i: (jnp.minimum(i, last), 0))
        self.out_shape = jax.ShapeDtypeStruct((rows, cols), BF16)


def _run_casts(srcs, dsts):
    for src, dst in zip(srcs, dsts):
        dst[...] = src[...].astype(BF16)


def _in_ret_kernel(*refs, tiles_per_seq, n_cast):
    (x_ref, g_ref, w_ref, cos_ref, sin_ref, dmask_ref, kdec_ref, qdec_ref, gch_ref,
     gn_ref) = refs[:10]
    cast_srcs = refs[10:10 + n_cast]
    u_ref, r_ref = refs[10 + n_cast:12 + n_cast]
    cast_dsts = refs[12 + n_cast:12 + 2 * n_cast]
    state_ref = refs[12 + 2 * n_cast]
    step = pl.program_id(0)

    @pl.when(step % tiles_per_seq == 0)
    def _():
        state_ref[...] = jnp.zeros_like(state_ref)

    _run_casts(cast_srcs, cast_dsts)
    n = _rms(x_ref[...], g_ref[...]).astype(BF16)
    proj = lambda c: _dot(n, w_ref[:, c * COL_GROUP:(c + 1) * COL_GROUP])
    n_attn = 3
    rq, rk, rv, rg = (proj(n_attn + j) for j in range(4))

    lane = lax.broadcasted_iota(jnp.int32, (RET_CHUNK, LANES), 1)
    even = (lane & 1) == 0

    def rot(x, cos, sin):
        partner = jnp.where(even, pltpu.roll(x, LANES - 1, 1), pltpu.roll(x, 1, 1))
        return x * cos + partner * sin

    heads = range(RET_HEADS)
    col = lambda h: slice(h * RET_HEAD_DIM, (h + 1) * RET_HEAD_DIM)
    state = [state_ref[h] for h in heads]
    n_chunks = IN_ROW_TILE // RET_CHUNK
    pieces = [(slice(0, IN_ROW_TILE), c) for c in range(n_attn)]
    while len(pieces) < n_chunks:
        rows, c = pieces.pop()
        mid = (rows.start + rows.stop) // 2
        pieces += [(slice(rows.start, mid), c), (slice(mid, rows.stop), c)]
    for c in range(max(n_chunks, len(pieces))):
        if c < len(pieces):
            rows, group = pieces[c]
            cols = slice(group * COL_GROUP, (group + 1) * COL_GROUP)
            u_ref[rows, cols] = _dot(n[rows], w_ref[:, cols]).astype(BF16)
        if c >= n_chunks:
            continue
        rows = slice(c * RET_CHUNK, (c + 1) * RET_CHUNK)
        cos, sin = cos_ref[rows, :], sin_ref[rows, :]
        q = [rot(rq[rows, col(h)], cos, sin) for h in heads]
        k = [rot(rk[rows, col(h)], cos, sin) * (RET_HEAD_DIM ** -0.5) for h in heads]
        v = [rv[rows, col(h)].astype(BF16) for h in heads]
        scores = [_dot_nt(q[h].astype(BF16), k[h].astype(BF16)) for h in heads]
        cross = [_dot((q[h] * qdec_ref[h]).astype(BF16), state[h].astype(BF16)) for h in heads]
        k_dec = [(k[h] * kdec_ref[h]).T.astype(BF16) for h in heads]
        state = [state[h] * gch_ref[h, 0:1, :] + _dot(k_dec[h], v[h]) for h in heads]
        y = [_dot((scores[h] * dmask_ref[h]).astype(BF16), v[h]) + cross[h] for h in heads]
        for h in heads:
            gate = rg[rows, col(h)]
            r_ref[rows, col(h)] = (gate * _sigmoid(gate) * _rms(y[h], gn_ref[h])).astype(BF16)
    for h in heads:
        state_ref[h] = state[h]


def _in_proj_retention(h, g, w, ret_norm_g, layer, seq_len, to_bf16):
    t = h.shape[0]
    steps = t // IN_ROW_TILE
    casts = [_Cast(w_f32, layer, steps) for w_f32 in to_bf16]
    c = RET_CHUNK
    nh = RET_HEADS
    assert ATTN_WIDTH == RET_WIDTH == COL_GROUP and seq_len % IN_ROW_TILE == 0 and IN_ROW_TILE % c == 0
    tiles_per_seq = seq_len // IN_ROW_TILE

    inv = 1.0 / (ROPE_BASE ** jnp.linspace(0.0, 1.0, RET_HEAD_DIM // 2, dtype=F32))
    ang = jnp.arange(seq_len)[:, None].astype(F32) * inv[None, :]
    cos_t = jnp.repeat(jnp.cos(ang), 2, axis=-1)
    sin_t = jnp.repeat(jnp.sin(ang), 2, axis=-1) * jnp.tile(jnp.array([-1.0, 1.0], F32), RET_HEAD_DIM // 2)
    log_g = jnp.log1p(-jnp.exp2(-5.0 - jnp.arange(nh, dtype=F32)))
    i = jnp.arange(c, dtype=F32)
    rel = i[:, None] - i[None, :]
    dmask = jnp.where(rel[None] >= 0, jnp.exp(jnp.maximum(rel, 0.0)[None] * log_g[:, None, None]), 0.0)
    k_dec = jnp.exp((c - 1 - i)[None, :] * log_g[:, None])
    q_dec = jnp.exp((i + 1.0)[None, :] * log_g[:, None])
    g_chunk = jnp.exp(c * log_g)
    k_dec = jnp.broadcast_to(k_dec[:, :, None], (nh, c, LANES))
    q_dec = jnp.broadcast_to(q_dec[:, :, None], (nh, c, LANES))
    g_chunk = jnp.broadcast_to(g_chunk[:, None, None], (nh, 8, LANES))

    row = lambda width: pl.BlockSpec((IN_ROW_TILE, width), lambda i: (i, 0))
    pos = pl.BlockSpec((IN_ROW_TILE, LANES), lambda i: (i % tiles_per_seq, 0))
    u, r, *converted = pl.pallas_call(
        functools.partial(_in_ret_kernel, tiles_per_seq=tiles_per_seq, n_cast=len(casts)),
        grid=(steps,),
        in_specs=[
            row(D_MODEL), _resident((1, D_MODEL), layer), _resident((D_MODEL, IN_WIDTH)),
            pos, pos,
            _resident((nh, c, c)), _resident((nh, c, LANES)), _resident((nh, c, LANES)),
            _resident((nh, 8, LANES)), _resident((nh, 1, RET_HEAD_DIM), layer),
        ] + [cast.in_spec for cast in casts],
        out_specs=[row(3 * ATTN_WIDTH), row(RET_WIDTH)] + [cast.out_spec for cast in casts],
        out_shape=[jax.ShapeDtypeStruct((t, 3 * ATTN_WIDTH), BF16),
                   jax.ShapeDtypeStruct((t, RET_WIDTH), BF16)] + [cast.out_shape for cast in casts],
        scratch_shapes=[pltpu.VMEM((nh, RET_HEAD_DIM, RET_HEAD_DIM), F32)],
        compiler_params=_params("arbitrary"),
        name="in_proj_retention",
    )(h, g, w, cos_t, sin_t, dmask, k_dec, q_dec, g_chunk, ret_norm_g, *to_bf16)
    return u, r, converted


ONES_ROWS = 16
VT_ROWS = ATTN_HEAD_DIM + ONES_ROWS
Q_BLOCKS = 16
Q_TILE = Q_BLOCKS * MOBA_BLOCK
PIPE_COLS = 2 * MOBA_BLOCK

def _moba_select(gate, q_block):
    nb = gate.shape[0]
    row = lax.broadcasted_iota(jnp.int32, gate.shape, 0)
    past = row < q_block
    g = jnp.where(past, gate, -jnp.inf)
    chosen = row == q_block
    for _ in range(MOBA_TOPK):
        top = jnp.max(g, axis=0, keepdims=True)
        hit = row == jnp.min(jnp.where(g == top, row, nb), axis=0, keepdims=True)
        chosen = chosen | (hit & past)
        g = jnp.where(hit, -jnp.inf, g)
    return jnp.where(chosen, 0.0, MASKED)


def _moba_kernel(q_ref, k_ref, v_ref, o_ref, kaug0_ref, kaug1_ref, vt0_ref, vt1_ref, kmean_ref,
                 sa_ref, sb_ref, *, nb):
    t = pl.program_id(2)
    blk = MOBA_BLOCK
    hd = ATTN_HEAD_DIM

    @pl.when(t == 0)
    def _():
        lane = lax.broadcasted_iota(jnp.int32, (blk, LANES), 1)
        head0_lane = lane < hd
        lane1 = lax.broadcasted_iota(jnp.int32, (1, LANES), 1)
        ones = jnp.ones((ONES_ROWS, blk), BF16)
        for n in range(nb):
            rows = slice(n * blk, (n + 1) * blk)
            kb = k_ref[0, rows, :]
            mean = jnp.sum(kb.astype(F32), axis=0, keepdims=True) * (1.0 / blk)
            kmean_ref[n:n + 1, :] = jnp.where(lane1 < hd, mean, 0.0)
            kmean_ref[nb + n:nb + n + 1, :] = jnp.where(lane1 < hd, 0.0, mean)
            kaug0_ref[n] = jnp.where(head0_lane, kb, (lane == hd + n).astype(BF16))
            kaug1_ref[n] = jnp.where(head0_lane, (lane == n).astype(BF16), kb)
            vt = v_ref[0, rows, :].T
            vt0_ref[n] = jnp.concatenate([vt[:hd], ones], axis=0)
            vt1_ref[n] = jnp.concatenate([vt[hd:], ones], axis=0)

    qt = (q_ref[0].astype(F32) * (hd ** -0.5 * LOG2_E)).T.astype(BF16)
    km = kmean_ref[...]
    km_hi = km.astype(BF16)
    km_lo = (km - km_hi.astype(F32)).astype(BF16)
    gate = _dot(km_hi, qt) + _dot(km_lo, qt)
    first = Q_BLOCKS * t
    zeros = lambda r: jnp.zeros((r, Q_TILE), BF16)
    kaug_refs = (kaug0_ref, kaug1_ref)
    vt_refs = (vt0_ref, vt1_ref)

    kpos = lax.broadcasted_iota(jnp.int32, (blk, blk), 0)
    qpos = lax.broadcasted_iota(jnp.int32, (blk, blk), 1)
    causal = kpos <= qpos
    group = lambda j: slice(j * blk, (j + 1) * blk)

    def softmax_pv(s, m_ref, vts):
        p = jnp.exp2(s - m_ref).astype(BF16)
        w = s.shape[1] // len(vts)
        parts = [_dot(vt, p[:, j * w:(j + 1) * w]) for j, vt in enumerate(vts)]
        return parts[0] if len(parts) == 1 else jnp.concatenate(parts, axis=1)

    qt_own = (jnp.concatenate([qt[:hd], zeros(LANES - hd)], axis=0),
              jnp.concatenate([zeros(hd), qt[hd:]], axis=0))
    for head in range(2):
        for j in range(Q_BLOCKS):
            logits = _dot(kaug_refs[head][first + j], qt_own[head][:, group(j)])
            sa_ref[head, :, group(j)] = jnp.where(causal, logits, MASKED)

    lane = lax.broadcasted_iota(jnp.int32, (1, Q_TILE), 1)
    q_block = first + sum((lane >= j * blk).astype(jnp.int32) for j in range(1, Q_BLOCKS))
    mask0 = _moba_select(gate[:nb], q_block).astype(BF16)
    mask1 = _moba_select(gate[nb:], q_block).astype(BF16)
    qt_sel = (jnp.concatenate([qt[:hd], mask0, zeros(LANES - hd - nb)], axis=0),
              jnp.concatenate([mask1, zeros(hd - nb), qt[hd:]], axis=0))

    def chunks(skip):
        width = Q_TILE - skip * blk
        return [(a, min(a + PIPE_COLS, width)) for a in range(0, width, PIPE_COLS)]

    def issue_chunk(s_ref, n, head, skip, a, b_):
        q0 = skip * blk
        s_ref[head, :, a:b_] = _dot(kaug_refs[head][n], qt_sel[head][:, q0 + a:q0 + b_])

    def issue_head(s_ref, n, head, skip=0):
        for a, b_ in chunks(skip):
            issue_chunk(s_ref, n, head, skip, a, b_)

    def consume_chunk(s_ref, n, head, m, acc, skip, a, b_):
        q0 = skip * blk
        s = s_ref[head, :, a:b_]
        m_old = m[:, q0 + a:q0 + b_]
        m_new = jnp.maximum(m_old, jnp.max(s, axis=0, keepdims=True))
        return m_new, (acc[:, q0 + a:q0 + b_] * jnp.exp2(m_old - m_new)
                       + softmax_pv(s, m_new, [vt_refs[head][n]]))

    def step(dst_ref, n_issue, src_ref, n_consume, carry, issue_skip=0, consume_skip=0):
        out = ()
        for head in range(2):
            m, acc = carry[2 * head], carry[2 * head + 1]
            q0 = consume_skip * blk
            m_cols = [m[:, :q0]] if q0 else []
            acc_cols = [acc[:, :q0]] if q0 else []
            todo_issue, todo_consume = chunks(issue_skip), chunks(consume_skip)
            for k in range(max(len(todo_issue), len(todo_consume))):
                if k < len(todo_issue):
                    issue_chunk(dst_ref, n_issue, head, issue_skip, *todo_issue[k])
                if k < len(todo_consume):
                    m_new, acc_new = consume_chunk(src_ref, n_consume, head, m, acc, consume_skip,
                                                   *todo_consume[k])
                    m_cols.append(m_new)
                    acc_cols.append(acc_new)
            out += (jnp.concatenate(m_cols, axis=1), jnp.concatenate(acc_cols, axis=1))
        return out

    bufs = (sb_ref, sa_ref)
    for head in range(2):
        issue_head(sb_ref, first, head, skip=1)
    carry = ()
    for head in range(2):
        s = sa_ref[head]
        m = jnp.max(s, axis=0, keepdims=True)
        carry += (m, softmax_pv(s, m, [vt_refs[head][first + j] for j in range(Q_BLOCKS)]))
    for j in range(Q_BLOCKS - 1):
        if j + 1 < Q_BLOCKS - 1:
            carry = step(bufs[(j + 1) % 2], first + j + 1, bufs[j % 2], first + j, carry,
                         issue_skip=j + 2, consume_skip=j + 1)
        else:
            carry = step(bufs[(j + 1) % 2], 0, bufs[j % 2], first + j, carry, consume_skip=j + 1)

    def two_blocks(n, carry):
        carry = step(sb_ref, n + 1, sa_ref, n, carry)
        return step(sa_ref, n + 2, sb_ref, n + 1, carry)

    carry = lax.fori_loop(
        0, first // 4, lambda j, c: two_blocks(4 * j + 2, two_blocks(4 * j, c)), carry)
    if Q_BLOCKS % 4:
        carry = lax.fori_loop(0, (first % 4) // 2, lambda _, c: two_blocks(first - 2, c), carry)
    _, a0, _, a1 = carry
    out_t = jnp.concatenate([a0[:hd] / a0[hd:hd + 1], a1[:hd] / a1[hd:hd + 1]], axis=0)
    o_ref[0] = out_t.T.astype(BF16)


def _moba(u):
    b, s, _ = u.shape
    nb = s // MOBA_BLOCK
    assert nb % ONES_ROWS == 0 and nb <= ATTN_HEAD_DIM and s % Q_TILE == 0 and Q_BLOCKS % 2 == 0
    pairs = ATTN_WIDTH // LANES
    return pl.pallas_call(
        functools.partial(_moba_kernel, nb=nb),
        grid=(b, pairs, s // Q_TILE),
        in_specs=[
            pl.BlockSpec((1, Q_TILE, LANES), lambda bi, hp, t: (bi, t, hp)),
            pl.BlockSpec((1, s, LANES), lambda bi, hp, t: (bi, 0, pairs + hp)),
            pl.BlockSpec((1, s, LANES), lambda bi, hp, t: (bi, 0, 2 * pairs + hp)),
        ],
        out_specs=pl.BlockSpec((1, Q_TILE, LANES), lambda bi, hp, t: (bi, t, hp)),
        out_shape=jax.ShapeDtypeStruct((b, s, ATTN_WIDTH), BF16),
        scratch_shapes=[
            pltpu.VMEM((nb, MOBA_BLOCK, LANES), BF16),
            pltpu.VMEM((nb, MOBA_BLOCK, LANES), BF16),
            pltpu.VMEM((nb, VT_ROWS, MOBA_BLOCK), BF16),
            pltpu.VMEM((nb, VT_ROWS, MOBA_BLOCK), BF16),
            pltpu.VMEM((2 * nb, LANES), F32),
            pltpu.VMEM((2, MOBA_BLOCK, Q_TILE), F32),
            pltpu.VMEM((2, MOBA_BLOCK, Q_TILE), F32),
        ],
        compiler_params=_params("parallel", "parallel", "arbitrary"),
        name="moba",
    )(u, u, u)


def _tail_kernel(*refs, final_norm, n_cast):
    (h_ref, a_ref, r_ref, p_ref, wo_ref, g_ffn_ref, w_in_ref, w_out_ref, g_ple_ref, wpg_ref,
     wpp_ref, g_final_ref) = refs[:12]
    cast_srcs = refs[12:12 + n_cast]
    o_ref = refs[12 + n_cast]
    cast_dsts = refs[13 + n_cast:13 + 2 * n_cast]
    act_ref = refs[13 + 2 * n_cast]
    _run_casts(cast_srcs, cast_dsts)
    halves = [slice(i * ROW_TILE // 2, (i + 1) * ROW_TILE // 2) for i in range(2)]
    h1 = [h_ref[rows, :] + _dot(a_ref[rows, :], wo_ref[:ATTN_WIDTH, :])
          + _dot(r_ref[rows, :], wo_ref[ATTN_WIDTH:, :]) for rows in halves]
    proj = [_dot(p_ref[rows, :].astype(BF16), wpp_ref[...]) for rows in halves]
    n1 = [_rms(x, g_ffn_ref[...]).astype(BF16) for x in h1]
    for n, rows in zip(n1, halves):
        for c in range(N_FF_CHUNKS):
            zg = _dot(n, w_in_ref[:, c * FF_CHUNK:(c + 1) * FF_CHUNK])
            zu = _dot(n, w_in_ref[:, D_FF + c * FF_CHUNK:D_FF + (c + 1) * FF_CHUNK])
            act_ref[rows, c * FF_CHUNK:(c + 1) * FF_CHUNK] = (zg * _sigmoid(zg) * zu).astype(BF16)
    h2 = [x + _dot(act_ref[rows, :], w_out_ref[...]) for x, rows in zip(h1, halves)]
    gate = [_sigmoid(_dot(_rms(x, g_ple_ref[...]).astype(BF16), wpg_ref[...])) for x in h2]
    for x, g, pr, rows in zip(h2, gate, proj, halves):
        out = x + g * pr
        if final_norm:
            out = _rms(out, g_final_ref[...])
        o_ref[rows, :] = out


def _tail(h, a, r, p, layer, wo, g_ffn, w_ffn_in, w_ffn_out, g_ple, wpg, wpp, g_final, final_norm,
          to_bf16):
    t = h.shape[0]
    steps = t // ROW_TILE
    casts = [_Cast(w_f32, cast_layer, steps) for w_f32, cast_layer in to_bf16]
    row = lambda w: pl.BlockSpec((ROW_TILE, w), lambda i: (i, 0))
    out, *converted = pl.pallas_call(
        functools.partial(_tail_kernel, final_norm=final_norm, n_cast=len(casts)),
        grid=(steps,),
        in_specs=[row(D_MODEL), row(ATTN_WIDTH), row(RET_WIDTH),
                  pl.BlockSpec((ROW_TILE, PLE_DIM), lambda i: (layer * steps + i, 0)),
                  _resident((ATTN_WIDTH + RET_WIDTH, D_MODEL)),
                  _resident((1, D_MODEL), layer), _resident((D_MODEL, 2 * D_FF)),
                  _resident((D_FF, D_MODEL)), _resident((1, D_MODEL), layer),
                  _resident((D_MODEL, D_MODEL)), _resident((PLE_DIM, D_MODEL)),
                  _resident((1, D_MODEL))] + [cast.in_spec for cast in casts],
        out_specs=[row(D_MODEL)] + [cast.out_spec for cast in casts],
        out_shape=[jax.ShapeDtypeStruct((t, D_MODEL), F32)] + [cast.out_shape for cast in casts],
        scratch_shapes=[pltpu.VMEM((ROW_TILE, D_FF), BF16)],
        compiler_params=_params("arbitrary"),
        name="tail",
    )(h, a, r, p, wo, g_ffn, w_ffn_in, w_ffn_out, g_ple, wpg, wpp, g_final,
      *[w_f32 for w_f32, _ in to_bf16])
    return out, converted


def kernel(x, p, attn_norm_g, w_in, ret_norm_g, w_out, ffn_norm_g, w_ffn_in, w_ffn_out,
           ple_norm_g, w_ple_gate, w_ple_proj, final_norm_g):
    b, s, d = x.shape
    depth = p.shape[0]
    assert d == D_MODEL and s % MOBA_BLOCK == 0 and s >= (MOBA_TOPK + 1) * MOBA_BLOCK
    assert (b * s) % ROW_TILE == 0 and s // MOBA_BLOCK <= LANES - ATTN_HEAD_DIM
    t = b * s
    h = x.reshape(t, d)
    gains = lambda g: g.reshape(depth, 1, -1).astype(F32)
    attn_g, ffn_g, ple_g = gains(attn_norm_g), gains(ffn_norm_g), gains(ple_norm_g)
    ret_g = ret_norm_g.reshape(depth, RET_HEADS, 1, RET_HEAD_DIM).astype(F32)
    p_rows = p.reshape(depth * t, PLE_DIM)
    tail_weights = [w_out, w_ffn_in, w_ffn_out, w_ple_gate, w_ple_proj]
    w_in_bf16 = w_in[0].astype(BF16)
    for i in range(depth):
        last = i == depth - 1
        u, r, tail_bf16 = _in_proj_retention(h, attn_g, w_in_bf16, ret_g, i, s, tail_weights)
        a = _moba(u.reshape(b, s, 3 * ATTN_WIDTH)).reshape(t, ATTN_WIDTH)
        wo, wf_in, wf_out, wpg, wpp = tail_bf16
        h, next_w_in = _tail(h, a, r, p_rows, i, wo, ffn_g, wf_in, wf_out, ple_g, wpg, wpp,
                             final_norm_g.reshape(1, -1).astype(F32), final_norm=last,
                             to_bf16=[] if last else [(w_in, i + 1)])
        if not last:
            (w_in_bf16,) = next_w_in
    return h.reshape(b, s, d)
```

```python
import functools

import jax
import jax.numpy as jnp
from jax import lax
from jax.experimental import pallas as pl
from jax.experimental.pallas import tpu as pltpu

D_MODEL = 1024
PLE_DIM = 256
ATTN_WIDTH = 512
ATTN_HEAD_DIM = 64
RET_WIDTH = 512
RET_HEADS = 4
RET_HEAD_DIM = 128
IN_WIDTH = 3 * ATTN_WIDTH + 4 * RET_WIDTH
MOBA_BLOCK = 256
MOBA_TOPK = 3
RET_CHUNK = 256
ROPE_BASE = 10000.0
D_FF = 2816
EPS = 1e-6

LANES = 128
FF_CHUNK = 256
N_FF_CHUNKS = D_FF // FF_CHUNK
ROW_TILE = 512
IN_ROW_TILE = 1024
COL_GROUP = 512
MASKED = -1e30
LOG2_E = 1.4426950408889634
VMEM_LIMIT = 52 * 1024 * 1024

F32 = jnp.float32
BF16 = jnp.bfloat16


def _dot(a, b):
    return jnp.dot(a, b, preferred_element_type=F32)


def _dot_nt(a, b):
    return lax.dot_general(a, b, (((1,), (1,)), ((), ())), preferred_element_type=F32)


def _rms(x, g):
    return x * lax.rsqrt(jnp.mean(x * x, axis=-1, keepdims=True) + EPS) * g


def _sigmoid(x):
    return 1.0 / (1.0 + jnp.exp(-x))


def _resident(shape, layer=None):
    zeros = (0,) * len(shape)
    if layer is None:
        return pl.BlockSpec(shape, lambda *_: zeros, pipeline_mode=pl.Buffered(1))
    return pl.BlockSpec((None,) + tuple(shape), lambda *_: (layer,) + zeros,
                        pipeline_mode=pl.Buffered(1))


def _params(*sem):
    return pltpu.CompilerParams(dimension_semantics=sem, vmem_limit_bytes=VMEM_LIMIT)


BF16_SUBLANES = 16


class _Cast:
    def __init__(self, w, layer, steps):
        rows, cols = w.shape[1:]
        per = -(-rows // steps)
        per = -(-per // BF16_SUBLANES) * BF16_SUBLANES
        while rows % per:
            per += BF16_SUBLANES
        last = rows // per - 1
        self.in_spec = pl.BlockSpec((None, per, cols), lambda i: (layer, jnp.minimum(i, last), 0))
        self.out_spec = pl.BlockSpec((per, cols), lambda i: (jnp.minimum(i, last), 0))
        self.out_shape = jax.ShapeDtypeStruct((rows, cols), BF16)


def _run_casts(srcs, dsts):
    for src, dst in zip(srcs, dsts):
        dst[...] = src[...].astype(BF16)


def _in_ret_kernel(*refs, tiles_per_seq, n_cast):
    (x_ref, g_ref, w_ref, cos_ref, sin_ref, dmask_ref, kdec_ref, qdec_ref, gch_ref,
     gn_ref) = refs[:10]
    cast_srcs = refs[10:10 + n_cast]
    u_ref, r_ref = refs[10 + n_cast:12 + n_cast]
    cast_dsts = refs[12 + n_cast:12 + 2 * n_cast]
    state_ref = refs[12 + 2 * n_cast]
    step = pl.program_id(0)

    @pl.when(step % tiles_per_seq == 0)
    def _():
        state_ref[...] = jnp.zeros_like(state_ref)

    _run_casts(cast_srcs, cast_dsts)
    n = _rms(x_ref[...], g_ref[...]).astype(BF16)
    proj = lambda c: _dot(n, w_ref[:, c * COL_GROUP:(c + 1) * COL_GROUP])
    n_attn = 3
    rq, rk, rv, rg = (proj(n_attn + j) for j in range(4))

    lane = lax.broadcasted_iota(jnp.int32, (RET_CHUNK, LANES), 1)
    even = (lane & 1) == 0

    def rot(x, cos, sin):
        partner = jnp.where(even, pltpu.roll(x, LANES - 1, 1), pltpu.roll(x, 1, 1))
        return x * cos + partner * sin

    heads = range(RET_HEADS)
    col = lambda h: slice(h * RET_HEAD_DIM, (h + 1) * RET_HEAD_DIM)
    state = [state_ref[h] for h in heads]
    n_chunks = IN_ROW_TILE // RET_CHUNK
    pieces = [(slice(0, IN_ROW_TILE), c) for c in range(n_attn)]
    while len(pieces) < n_chunks:
        rows, c = pieces.pop()
        mid = (rows.start + rows.stop) // 2
        pieces += [(slice(rows.start, mid), c), (slice(mid, rows.stop), c)]
    for c in range(max(n_chunks, len(pieces))):
        if c < len(pieces):
            rows, group = pieces[c]
            cols = slice(group * COL_GROUP, (group + 1) * COL_GROUP)
            u_ref[rows, cols] = _dot(n[rows], w_ref[:, cols]).astype(BF16)
        if c >= n_chunks:
            continue
        rows = slice(c * RET_CHUNK, (c + 1) * RET_CHUNK)
        cos, sin = cos_ref[rows, :], sin_ref[rows, :]
        q = [rot(rq[rows, col(h)], cos, sin) for h in heads]
        k = [rot(rk[rows, col(h)], cos, sin) * (RET_HEAD_DIM ** -0.5) for h in heads]
        v = [rv[rows, col(h)].astype(BF16) for h in heads]
        scores = [_dot_nt(q[h].astype(BF16), k[h].astype(BF16)) for h in heads]
        cross = [_dot((q[h] * qdec_ref[h]).astype(BF16), state[h].astype(BF16)) for h in heads]
        k_dec = [(k[h] * kdec_ref[h]).T.astype(BF16) for h in heads]
        state = [state[h] * gch_ref[h, 0:1, :] + _dot(k_dec[h], v[h]) for h in heads]
        y = [_dot((scores[h] * dmask_ref[h]).astype(BF16), v[h]) + cross[h] for h in heads]
        for h in heads:
            gate = rg[rows, col(h)]
            r_ref[rows, col(h)] = (gate * _sigmoid(gate) * _rms(y[h], gn_ref[h])).astype(BF16)
    for h in heads:
        state_ref[h] = state[h]


def _in_proj_retention(h, g, w, ret_norm_g, layer, seq_len, to_bf16):
    t = h.shape[0]
    steps = t // IN_ROW_TILE
    casts = [_Cast(w_f32, layer, steps) for w_f32 in to_bf16]
    c = RET_CHUNK
    nh = RET_HEADS
    assert ATTN_WIDTH == RET_WIDTH == COL_GROUP and seq_len % IN_ROW_TILE == 0 and IN_ROW_TILE % c == 0
    tiles_per_seq = seq_len // IN_ROW_TILE

    inv = 1.0 / (ROPE_BASE ** jnp.linspace(0.0, 1.0, RET_HEAD_DIM // 2, dtype=F32))
    ang = jnp.arange(seq_len)[:, None].astype(F32) * inv[None, :]
    cos_t = jnp.repeat(jnp.cos(ang), 2, axis=-1)
    sin_t = jnp.repeat(jnp.sin(ang), 2, axis=-1) * jnp.tile(jnp.array([-1.0, 1.0], F32), RET_HEAD_DIM // 2)
    log_g = jnp.log1p(-jnp.exp2(-5.0 - jnp.arange(nh, dtype=F32)))
    i = jnp.arange(c, dtype=F32)
    rel = i[:, None] - i[None, :]
    dmask = jnp.where(rel[None] >= 0, jnp.exp(jnp.maximum(rel, 0.0)[None] * log_g[:, None, None]), 0.0)
    k_dec = jnp.exp((c - 1 - i)[None, :] * log_g[:, None])
    q_dec = jnp.exp((i + 1.0)[None, :] * log_g[:, None])
    g_chunk = jnp.exp(c * log_g)
    k_dec = jnp.broadcast_to(k_dec[:, :, None], (nh, c, LANES))
    q_dec = jnp.broadcast_to(q_dec[:, :, None], (nh, c, LANES))
    g_chunk = jnp.broadcast_to(g_chunk[:, None, None], (nh, 8, LANES))

    row = lambda width: pl.BlockSpec((IN_ROW_TILE, width), lambda i: (i, 0))
    pos = pl.BlockSpec((IN_ROW_TILE, LANES), lambda i: (i % tiles_per_seq, 0))
    u, r, *converted = pl.pallas_call(
        functools.partial(_in_ret_kernel, tiles_per_seq=tiles_per_seq, n_cast=len(casts)),
        grid=(steps,),
        in_specs=[
            row(D_MODEL), _resident((1, D_MODEL), layer), _resident((D_MODEL, IN_WIDTH)),
            pos, pos,
            _resident((nh, c, c)), _resident((nh, c, LANES)), _resident((nh, c, LANES)),
            _resident((nh, 8, LANES)), _resident((nh, 1, RET_HEAD_DIM), layer),
        ] + [cast.in_spec for cast in casts],
        out_specs=[row(3 * ATTN_WIDTH), row(RET_WIDTH)] + [cast.out_spec for cast in casts],
        out_shape=[jax.ShapeDtypeStruct((t, 3 * ATTN_WIDTH), BF16),
                   jax.ShapeDtypeStruct((t, RET_WIDTH), BF16)] + [cast.out_shape for cast in casts],
        scratch_shapes=[pltpu.VMEM((nh, RET_HEAD_DIM, RET_HEAD_DIM), F32)],
        compiler_params=_params("arbitrary"),
        name="in_proj_retention",
    )(h, g, w, cos_t, sin_t, dmask, k_dec, q_dec, g_chunk, ret_norm_g, *to_bf16)
    return u, r, converted


ONES_ROWS = 16
VT_ROWS = ATTN_HEAD_DIM + ONES_ROWS
Q_BLOCKS = 16
Q_TILE = Q_BLOCKS * MOBA_BLOCK
PIPE_COLS = 2 * MOBA_BLOCK

def _moba_select(gate, q_block):
    nb = gate.shape[0]
    row = lax.broadcasted_iota(jnp.int32, gate.shape, 0)
    past = row < q_block
    g = jnp.where(past, gate, -jnp.inf)
    chosen = row == q_block
    for _ in range(MOBA_TOPK):
        top = jnp.max(g, axis=0, keepdims=True)
        hit = row == jnp.min(jnp.where(g == top, row, nb), axis=0, keepdims=True)
        chosen = chosen | (hit & past)
        g = jnp.where(hit, -jnp.inf, g)
    return jnp.where(chosen, 0.0, MASKED)


def _moba_kernel(q_ref, k_ref, v_ref, o_ref, kaug0_ref, kaug1_ref, vt0_ref, vt1_ref, kmean_ref,
                 sa_ref, sb_ref, *, nb):
    t = pl.program_id(2)
    blk = MOBA_BLOCK
    hd = ATTN_HEAD_DIM

    @pl.when(t == 0)
    def _():
        lane = lax.broadcasted_iota(jnp.int32, (blk, LANES), 1)
        head0_lane = lane < hd
        lane1 = lax.broadcasted_iota(jnp.int32, (1, LANES), 1)
        ones = jnp.ones((ONES_ROWS, blk), BF16)
        for n in range(nb):
            rows = slice(n * blk, (n + 1) * blk)
            kb = k_ref[0, rows, :]
            mean = jnp.sum(kb.astype(F32), axis=0, keepdims=True) * (1.0 / blk)
            kmean_ref[n:n + 1, :] = jnp.where(lane1 < hd, mean, 0.0)
            kmean_ref[nb + n:nb + n + 1, :] = jnp.where(lane1 < hd, 0.0, mean)
            kaug0_ref[n] = jnp.where(head0_lane, kb, (lane == hd + n).astype(BF16))
            kaug1_ref[n] = jnp.where(head0_lane, (lane == n).astype(BF16), kb)
            vt = v_ref[0, rows, :].T
            vt0_ref[n] = jnp.concatenate([vt[:hd], ones], axis=0)
            vt1_ref[n] = jnp.concatenate([vt[hd:], ones], axis=0)

    qt = (q_ref[0].astype(F32) * (hd ** -0.5 * LOG2_E)).T.astype(BF16)
    km = kmean_ref[...]
    km_hi = km.astype(BF16)
    km_lo = (km - km_hi.astype(F32)).astype(BF16)
    gate = _dot(km_hi, qt) + _dot(km_lo, qt)
    first = Q_BLOCKS * t
    has_past = nb > Q_BLOCKS
    zeros = lambda r: jnp.zeros((r, Q_TILE), BF16)
    kaug_refs = (kaug0_ref, kaug1_ref)
    vt_refs = (vt0_ref, vt1_ref)

    kpos = lax.broadcasted_iota(jnp.int32, (blk, blk), 0)
    qpos = lax.broadcasted_iota(jnp.int32, (blk, blk), 1)
    causal = kpos <= qpos
    group = lambda j: slice(j * blk, (j + 1) * blk)

    def softmax_pv(s, m_ref, vts):
        p = jnp.exp2(s - m_ref).astype(BF16)
        w = s.shape[1] // len(vts)
        parts = [_dot(vt, p[:, j * w:(j + 1) * w]) for j, vt in enumerate(vts)]
        return parts[0] if len(parts) == 1 else jnp.concatenate(parts, axis=1)

    qt_own = (jnp.concatenate([qt[:hd], zeros(LANES - hd)], axis=0),
              jnp.concatenate([zeros(hd), qt[hd:]], axis=0))
    for head in range(2):
        for j in range(Q_BLOCKS):
            logits = _dot(kaug_refs[head][first + j], qt_own[head][:, group(j)])
            sa_ref[head, :, group(j)] = jnp.where(causal, logits, MASKED)

    lane = lax.broadcasted_iota(jnp.int32, (1, Q_TILE), 1)
    q_block = first + sum((lane >= j * blk).astype(jnp.int32) for j in range(1, Q_BLOCKS))
    mask0 = _moba_select(gate[:nb], q_block).astype(BF16)
    mask1 = _moba_select(gate[nb:], q_block).astype(BF16)
    qt_sel = (jnp.concatenate([qt[:hd], mask0, zeros(LANES - hd - nb)], axis=0),
              jnp.concatenate([mask1, zeros(hd - nb), qt[hd:]], axis=0))

    def chunks(skip):
        width = Q_TILE - skip * blk
        return [(a, min(a + PIPE_COLS, width)) for a in range(0, width, PIPE_COLS)]

    def issue_chunk(s_ref, n, head, skip, a, b_):
        q0 = skip * blk
        s_ref[head, :, a:b_] = _dot(kaug_refs[head][n], qt_sel[head][:, q0 + a:q0 + b_])

    def issue_head(s_ref, n, head, skip=0):
        for a, b_ in chunks(skip):
            issue_chunk(s_ref, n, head, skip, a, b_)

    def consume_chunk(s_ref, n, head, m, acc, skip, a, b_):
        q0 = skip * blk
        s = s_ref[head, :, a:b_]
        m_old = m[:, q0 + a:q0 + b_]
        m_new = jnp.maximum(m_old, jnp.max(s, axis=0, keepdims=True))
        return m_new, (acc[:, q0 + a:q0 + b_] * jnp.exp2(m_old - m_new)
                       + softmax_pv(s, m_new, [vt_refs[head][n]]))

    def step(dst_ref, n_issue, src_ref, n_consume, carry, issue_skip=0, consume_skip=0):
        out = ()
        for head in range(2):
            m, acc = carry[2 * head], carry[2 * head + 1]
            q0 = consume_skip * blk
            m_cols = [m[:, :q0]] if q0 else []
            acc_cols = [acc[:, :q0]] if q0 else []
            todo_issue = chunks(issue_skip) if n_issue is not None else []
            todo_consume = chunks(consume_skip)
            for k in range(max(len(todo_issue), len(todo_consume))):
                if k < len(todo_issue):
                    issue_chunk(dst_ref, n_issue, head, issue_skip, *todo_issue[k])
                if k < len(todo_consume):
                    m_new, acc_new = consume_chunk(src_ref, n_consume, head, m, acc, consume_skip,
                                                   *todo_consume[k])
                    m_cols.append(m_new)
                    acc_cols.append(acc_new)
            out += (jnp.concatenate(m_cols, axis=1), jnp.concatenate(acc_cols, axis=1))
        return out

    bufs = (sb_ref, sa_ref)
    for head in range(2):
        issue_head(sb_ref, first, head, skip=1)
    carry = ()
    for head in range(2):
        s = sa_ref[head]
        m = jnp.max(s, axis=0, keepdims=True)
        carry += (m, softmax_pv(s, m, [vt_refs[head][first + j] for j in range(Q_BLOCKS)]))
    for j in range(Q_BLOCKS - 1):
        if j + 1 < Q_BLOCKS - 1:
            carry = step(bufs[(j + 1) % 2], first + j + 1, bufs[j % 2], first + j, carry,
                         issue_skip=j + 2, consume_skip=j + 1)
        else:
            carry = step(bufs[(j + 1) % 2], 0 if has_past else None, bufs[j % 2], first + j, carry,
                         consume_skip=j + 1)

    def two_blocks(n, carry):
        carry = step(sb_ref, n + 1, sa_ref, n, carry)
        return step(sa_ref, n + 2, sb_ref, n + 1, carry)

    if has_past:
        carry = lax.fori_loop(
            0, first // 4, lambda j, c: two_blocks(4 * j + 2, two_blocks(4 * j, c)), carry)
        if Q_BLOCKS % 4:
            carry = lax.fori_loop(0, (first % 4) // 2, lambda _, c: two_blocks(first - 2, c), carry)
    _, a0, _, a1 = carry
    out_t = jnp.concatenate([a0[:hd] / a0[hd:hd + 1], a1[:hd] / a1[hd:hd + 1]], axis=0)
    o_ref[0] = out_t.T.astype(BF16)


def _moba(u):
    b, s, _ = u.shape
    nb = s // MOBA_BLOCK
    assert nb % ONES_ROWS == 0 and nb <= ATTN_HEAD_DIM and s % Q_TILE == 0 and Q_BLOCKS % 2 == 0
    pairs = ATTN_WIDTH // LANES
    return pl.pallas_call(
        functools.partial(_moba_kernel, nb=nb),
        grid=(b, pairs, s // Q_TILE),
        in_specs=[
            pl.BlockSpec((1, Q_TILE, LANES), lambda bi, hp, t: (bi, t, hp)),
            pl.BlockSpec((1, s, LANES), lambda bi, hp, t: (bi, 0, pairs + hp)),
            pl.BlockSpec((1, s, LANES), lambda bi, hp, t: (bi, 0, 2 * pairs + hp)),
        ],
        out_specs=pl.BlockSpec((1, Q_TILE, LANES), lambda bi, hp, t: (bi, t, hp)),
        out_shape=jax.ShapeDtypeStruct((b, s, ATTN_WIDTH), BF16),
        scratch_shapes=[
            pltpu.VMEM((nb, MOBA_BLOCK, LANES), BF16),
            pltpu.VMEM((nb, MOBA_BLOCK, LANES), BF16),
            pltpu.VMEM((nb, VT_ROWS, MOBA_BLOCK), BF16),
            pltpu.VMEM((nb, VT_ROWS, MOBA_BLOCK), BF16),
            pltpu.VMEM((2 * nb, LANES), F32),
            pltpu.VMEM((2, MOBA_BLOCK, Q_TILE), F32),
            pltpu.VMEM((2, MOBA_BLOCK, Q_TILE), F32),
        ],
        compiler_params=_params("parallel", "parallel", "arbitrary"),
        name="moba",
    )(u, u, u)


def _tail_kernel(*refs, final_norm, n_cast):
    (h_ref, a_ref, r_ref, p_ref, wo_ref, g_ffn_ref, w_in_ref, w_out_ref, g_ple_ref, wpg_ref,
     wpp_ref, g_final_ref) = refs[:12]
    cast_srcs = refs[12:12 + n_cast]
    o_ref = refs[12 + n_cast]
    cast_dsts = refs[13 + n_cast:13 + 2 * n_cast]
    act_ref = refs[13 + 2 * n_cast]
    _run_casts(cast_srcs, cast_dsts)
    halves = [slice(i * ROW_TILE // 2, (i + 1) * ROW_TILE // 2) for i in range(2)]
    h1 = [h_ref[rows, :] + _dot(a_ref[rows, :], wo_ref[:ATTN_WIDTH, :])
          + _dot(r_ref[rows, :], wo_ref[ATTN_WIDTH:, :]) for rows in halves]
    proj = [_dot(p_ref[rows, :].astype(BF16), wpp_ref[...]) for rows in halves]
    n1 = [_rms(x, g_ffn_ref[...]).astype(BF16) for x in h1]
    for n, rows in zip(n1, halves):
        for c in range(N_FF_CHUNKS):
            zg = _dot(n, w_in_ref[:, c * FF_CHUNK:(c + 1) * FF_CHUNK])
            zu = _dot(n, w_in_ref[:, D_FF + c * FF_CHUNK:D_FF + (c + 1) * FF_CHUNK])
            act_ref[rows, c * FF_CHUNK:(c + 1) * FF_CHUNK] = (zg * _sigmoid(zg) * zu).astype(BF16)
    h2 = [x + _dot(act_ref[rows, :], w_out_ref[...]) for x, rows in zip(h1, halves)]
    gate = [_sigmoid(_dot(_rms(x, g_ple_ref[...]).astype(BF16), wpg_ref[...])) for x in h2]
    for x, g, pr, rows in zip(h2, gate, proj, halves):
        out = x + g * pr
        if final_norm:
            out = _rms(out, g_final_ref[...])
        o_ref[rows, :] = out


def _tail(h, a, r, p, layer, wo, g_ffn, w_ffn_in, w_ffn_out, g_ple, wpg, wpp, g_final, final_norm,
          to_bf16):
    t = h.shape[0]
    steps = t // ROW_TILE
    casts = [_Cast(w_f32, cast_layer, steps) for w_f32, cast_layer in to_bf16]
    row = lambda w: pl.BlockSpec((ROW_TILE, w), lambda i: (i, 0))
    out, *converted = pl.pallas_call(
        functools.partial(_tail_kernel, final_norm=final_norm, n_cast=len(casts)),
        grid=(steps,),
        in_specs=[row(D_MODEL), row(ATTN_WIDTH), row(RET_WIDTH),
                  pl.BlockSpec((ROW_TILE, PLE_DIM), lambda i: (layer * steps + i, 0)),
                  _resident((ATTN_WIDTH + RET_WIDTH, D_MODEL)),
                  _resident((1, D_MODEL), layer), _resident((D_MODEL, 2 * D_FF)),
                  _resident((D_FF, D_MODEL)), _resident((1, D_MODEL), layer),
                  _resident((D_MODEL, D_MODEL)), _resident((PLE_DIM, D_MODEL)),
                  _resident((1, D_MODEL))] + [cast.in_spec for cast in casts],
        out_specs=[row(D_MODEL)] + [cast.out_spec for cast in casts],
        out_shape=[jax.ShapeDtypeStruct((t, D_MODEL), F32)] + [cast.out_shape for cast in casts],
        scratch_shapes=[pltpu.VMEM((ROW_TILE, D_FF), BF16)],
        compiler_params=_params("arbitrary"),
        name="tail",
    )(h, a, r, p, wo, g_ffn, w_ffn_in, w_ffn_out, g_ple, wpg, wpp, g_final,
      *[w_f32 for w_f32, _ in to_bf16])
    return out, converted


def kernel(x, p, attn_norm_g, w_in, ret_norm_g, w_out, ffn_norm_g, w_ffn_in, w_ffn_out,
           ple_norm_g, w_ple_gate, w_ple_proj, final_norm_g):
    b, s, d = x.shape
    depth = p.shape[0]
    assert d == D_MODEL and s % MOBA_BLOCK == 0 and s >= (MOBA_TOPK + 1) * MOBA_BLOCK
    assert (b * s) % ROW_TILE == 0 and s // MOBA_BLOCK <= LANES - ATTN_HEAD_DIM
    t = b * s
    h = x.reshape(t, d)
    gains = lambda g: g.reshape(depth, 1, -1).astype(F32)
    attn_g, ffn_g, ple_g = gains(attn_norm_g), gains(ffn_norm_g), gains(ple_norm_g)
    ret_g = ret_norm_g.reshape(depth, RET_HEADS, 1, RET_HEAD_DIM).astype(F32)
    p_rows = p.reshape(depth * t, PLE_DIM)
    tail_weights = [w_out, w_ffn_in, w_ffn_out, w_ple_gate, w_ple_proj]
    w_in_bf16 = w_in[0].astype(BF16)
    for i in range(depth):
        last = i == depth - 1
        u, r, tail_bf16 = _in_proj_retention(h, attn_g, w_in_bf16, ret_g, i, s, tail_weights)
        a = _moba(u.reshape(b, s, 3 * ATTN_WIDTH)).reshape(t, ATTN_WIDTH)
        wo, wf_in, wf_out, wpg, wpp = tail_bf16
        h, next_w_in = _tail(h, a, r, p_rows, i, wo, ffn_g, wf_in, wf_out, ple_g, wpg, wpp,
                             final_norm_g.reshape(1, -1).astype(F32), final_norm=last,
                             to_bf16=[] if last else [(w_in, i + 1)])
        if not last:
            (w_in_bf16,) = next_w_in
    return h.reshape(b, s, d)
```

```python
import functools

import jax
import jax.numpy as jnp
from jax import lax
from jax.experimental import pallas as pl
from jax.experimental.pallas import tpu as pltpu

D_MODEL = 1024
PLE_DIM = 256
ATTN_WIDTH = 512
ATTN_HEAD_DIM = 64
RET_WIDTH = 512
RET_HEADS = 4
RET_HEAD_DIM = 128
IN_WIDTH = 3 * ATTN_WIDTH + 4 * RET_WIDTH
MOBA_BLOCK = 256
MOBA_TOPK = 3
RET_CHUNK = 256
ROPE_BASE = 10000.0
D_FF = 2816
EPS = 1e-6

LANES = 128
FF_CHUNK = 256
N_FF_CHUNKS = D_FF // FF_CHUNK
ROW_TILE = 512
IN_ROW_TILE = 1024
COL_GROUP = 512
MASKED = -1e30
LOG2_E = 1.4426950408889634
VMEM_LIMIT = 52 * 1024 * 1024

F32 = jnp.float32
BF16 = jnp.bfloat16


def _dot(a, b):
    return jnp.dot(a, b, preferred_element_type=F32)


def _dot_nt(a, b):
    return lax.dot_general(a, b, (((1,), (1,)), ((), ())), preferred_element_type=F32)


def _rms(x, g):
    return x * lax.rsqrt(jnp.mean(x * x, axis=-1, keepdims=True) + EPS) * g


def _sigmoid(x):
    return 1.0 / (1.0 + jnp.exp(-x))


def _resident(shape, layer=None):
    zeros = (0,) * len(shape)
    if layer is None:
        return pl.BlockSpec(shape, lambda *_: zeros, pipeline_mode=pl.Buffered(1))
    return pl.BlockSpec((None,) + tuple(shape), lambda *_: (layer,) + zeros,
                        pipeline_mode=pl.Buffered(1))


def _params(*sem):
    return pltpu.CompilerParams(dimension_semantics=sem, vmem_limit_bytes=VMEM_LIMIT)


BF16_SUBLANES = 16


class _Cast:
    def __init__(self, w, layer, steps):
        rows, cols = w.shape[1:]
        per = -(-rows // steps)
        per = -(-per // BF16_SUBLANES) * BF16_SUBLANES
        while rows % per:
            per += BF16_SUBLANES
        last = rows // per - 1
        self.in_spec = pl.BlockSpec((None, per, cols), lambda i: (layer, jnp.minimum(i, last), 0))
        self.out_spec = pl.BlockSpec((per, cols), lambda i: (jnp.minimum(i, last), 0))
        self.out_shape = jax.ShapeDtypeStruct((rows, cols), BF16)


def _run_casts(srcs, dsts):
    for src, dst in zip(srcs, dsts):
        dst[...] = src[...].astype(BF16)


def _in_ret_kernel(*refs, tiles_per_seq, n_cast):
    (x_ref, g_ref, w_ref, cos_ref, sin_ref, dmask_ref, kdec_ref, qdec_ref, gch_ref,
     gn_ref) = refs[:10]
    cast_srcs = refs[10:10 + n_cast]
    u_ref, r_ref = refs[10 + n_cast:12 + n_cast]
    cast_dsts = refs[12 + n_cast:12 + 2 * n_cast]
    state_ref = refs[12 + 2 * n_cast]
    step = pl.program_id(0)

    @pl.when(step % tiles_per_seq == 0)
    def _():
        state_ref[...] = jnp.zeros_like(state_ref)

    _run_casts(cast_srcs, cast_dsts)
    n = _rms(x_ref[...], g_ref[...]).astype(BF16)
    proj = lambda c: _dot(n, w_ref[:, c * COL_GROUP:(c + 1) * COL_GROUP])
    n_attn = 3
    rq, rk, rv, rg = (proj(n_attn + j) for j in range(4))

    lane = lax.broadcasted_iota(jnp.int32, (RET_CHUNK, LANES), 1)
    even = (lane & 1) == 0

    def rot(x, cos, sin):
        partner = jnp.where(even, pltpu.roll(x, LANES - 1, 1), pltpu.roll(x, 1, 1))
        return x * cos + partner * sin

    heads = range(RET_HEADS)
    col = lambda h: slice(h * RET_HEAD_DIM, (h + 1) * RET_HEAD_DIM)
    state = [state_ref[h] for h in heads]
    n_chunks = IN_ROW_TILE // RET_CHUNK
    pieces = [(slice(0, IN_ROW_TILE), c) for c in range(n_attn)]
    while len(pieces) < n_chunks:
        rows, c = pieces.pop()
        mid = (rows.start + rows.stop) // 2
        pieces += [(slice(rows.start, mid), c), (slice(mid, rows.stop), c)]
    for c in range(max(n_chunks, len(pieces))):
        if c < len(pieces):
            rows, group = pieces[c]
            cols = slice(group * COL_GROUP, (group + 1) * COL_GROUP)
            u_ref[rows, cols] = _dot(n[rows], w_ref[:, cols]).astype(BF16)
        if c >= n_chunks:
            continue
        rows = slice(c * RET_CHUNK, (c + 1) * RET_CHUNK)
        cos, sin = cos_ref[rows, :], sin_ref[rows, :]
        q = [rot(rq[rows, col(h)], cos, sin) for h in heads]
        k = [rot(rk[rows, col(h)], cos, sin) * (RET_HEAD_DIM ** -0.5) for h in heads]
        v = [rv[rows, col(h)].astype(BF16) for h in heads]
        scores = [_dot_nt(q[h].astype(BF16), k[h].astype(BF16)) for h in heads]
        cross = [_dot((q[h] * qdec_ref[h]).astype(BF16), state[h].astype(BF16)) for h in heads]
        k_dec = [(k[h] * kdec_ref[h]).T.astype(BF16) for h in heads]
        state = [state[h] * gch_ref[h, 0:1, :] + _dot(k_dec[h], v[h]) for h in heads]
        y = [_dot((scores[h] * dmask_ref[h]).astype(BF16), v[h]) + cross[h] for h in heads]
        for h in heads:
            gate = rg[rows, col(h)]
            r_ref[rows, col(h)] = (gate * _sigmoid(gate) * _rms(y[h], gn_ref[h])).astype(BF16)
    for h in heads:
        state_ref[h] = state[h]


def _in_proj_retention(h, g, w, ret_norm_g, layer, seq_len, to_bf16):
    t = h.shape[0]
    steps = t // IN_ROW_TILE
    casts = [_Cast(w_f32, layer, steps) for w_f32 in to_bf16]
    c = RET_CHUNK
    nh = RET_HEADS
    assert ATTN_WIDTH == RET_WIDTH == COL_GROUP and seq_len % IN_ROW_TILE == 0 and IN_ROW_TILE % c == 0
    tiles_per_seq = seq_len // IN_ROW_TILE

    inv = 1.0 / (ROPE_BASE ** jnp.linspace(0.0, 1.0, RET_HEAD_DIM // 2, dtype=F32))
    ang = jnp.arange(seq_len)[:, None].astype(F32) * inv[None, :]
    cos_t = jnp.repeat(jnp.cos(ang), 2, axis=-1)
    sin_t = jnp.repeat(jnp.sin(ang), 2, axis=-1) * jnp.tile(jnp.array([-1.0, 1.0], F32), RET_HEAD_DIM // 2)
    log_g = jnp.log1p(-jnp.exp2(-5.0 - jnp.arange(nh, dtype=F32)))
    i = jnp.arange(c, dtype=F32)
    rel = i[:, None] - i[None, :]
    dmask = jnp.where(rel[None] >= 0, jnp.exp(jnp.maximum(rel, 0.0)[None] * log_g[:, None, None]), 0.0)
    k_dec = jnp.exp((c - 1 - i)[None, :] * log_g[:, None])
    q_dec = jnp.exp((i + 1.0)[None, :] * log_g[:, None])
    g_chunk = jnp.exp(c * log_g)
    k_dec = jnp.broadcast_to(k_dec[:, :, None], (nh, c, LANES))
    q_dec = jnp.broadcast_to(q_dec[:, :, None], (nh, c, LANES))
    g_chunk = jnp.broadcast_to(g_chunk[:, None, None], (nh, 8, LANES))

    row = lambda width: pl.BlockSpec((IN_ROW_TILE, width), lambda i: (i, 0))
    pos = pl.BlockSpec((IN_ROW_TILE, LANES), lambda i: (i % tiles_per_seq, 0))
    u, r, *converted = pl.pallas_call(
        functools.partial(_in_ret_kernel, tiles_per_seq=tiles_per_seq, n_cast=len(casts)),
        grid=(steps,),
        in_specs=[
            row(D_MODEL), _resident((1, D_MODEL), layer), _resident((D_MODEL, IN_WIDTH)),
            pos, pos,
            _resident((nh, c, c)), _resident((nh, c, LANES)), _resident((nh, c, LANES)),
            _resident((nh, 8, LANES)), _resident((nh, 1, RET_HEAD_DIM), layer),
        ] + [cast.in_spec for cast in casts],
        out_specs=[row(3 * ATTN_WIDTH), row(RET_WIDTH)] + [cast.out_spec for cast in casts],
        out_shape=[jax.ShapeDtypeStruct((t, 3 * ATTN_WIDTH), BF16),
                   jax.ShapeDtypeStruct((t, RET_WIDTH), BF16)] + [cast.out_shape for cast in casts],
        scratch_shapes=[pltpu.VMEM((nh, RET_HEAD_DIM, RET_HEAD_DIM), F32)],
        compiler_params=_params("arbitrary"),
        name="in_proj_retention",
    )(h, g, w, cos_t, sin_t, dmask, k_dec, q_dec, g_chunk, ret_norm_g, *to_bf16)
    return u, r, converted


ONES_ROWS = 16
VT_ROWS = ATTN_HEAD_DIM + ONES_ROWS
Q_BLOCKS = 16
Q_TILE = Q_BLOCKS * MOBA_BLOCK
PIPE_COLS = 2 * MOBA_BLOCK

def _moba_select(gate, q_block):
    nb = gate.shape[0]
    row = lax.broadcasted_iota(jnp.int32, gate.shape, 0)
    past = row < q_block
    g = jnp.where(past, gate, -jnp.inf)
    chosen = row == q_block
    for _ in range(MOBA_TOPK):
        top = jnp.max(g, axis=0, keepdims=True)
        hit = row == jnp.min(jnp.where(g == top, row, nb), axis=0, keepdims=True)
        chosen = chosen | (hit & past)
        g = jnp.where(hit, -jnp.inf, g)
    return jnp.where(chosen, 0.0, MASKED)


def _moba_kernel(q_ref, k_ref, v_ref, o_ref, kaug0_ref, kaug1_ref, vt0_ref, vt1_ref, kmean_ref,
                 sa_ref, sb_ref, *, nb):
    t = pl.program_id(2)
    blk = MOBA_BLOCK
    hd = ATTN_HEAD_DIM

    @pl.when(t == 0)
    def _():
        lane = lax.broadcasted_iota(jnp.int32, (blk, LANES), 1)
        head0_lane = lane < hd
        lane1 = lax.broadcasted_iota(jnp.int32, (1, LANES), 1)
        ones = jnp.ones((ONES_ROWS, blk), BF16)
        for n in range(nb):
            rows = slice(n * blk, (n + 1) * blk)
            kb = k_ref[0, rows, :]
            mean = jnp.sum(kb.astype(F32), axis=0, keepdims=True) * (1.0 / blk)
            kmean_ref[n:n + 1, :] = jnp.where(lane1 < hd, mean, 0.0)
            kmean_ref[nb + n:nb + n + 1, :] = jnp.where(lane1 < hd, 0.0, mean)
            kaug0_ref[n] = jnp.where(head0_lane, kb, (lane == hd + n).astype(BF16))
            kaug1_ref[n] = jnp.where(head0_lane, (lane == n).astype(BF16), kb)
            vt = v_ref[0, rows, :].T
            vt0_ref[n] = jnp.concatenate([vt[:hd], ones], axis=0)
            vt1_ref[n] = jnp.concatenate([vt[hd:], ones], axis=0)

    qt = (q_ref[0].astype(F32) * (hd ** -0.5 * LOG2_E)).T.astype(BF16)
    km = kmean_ref[...]
    km_hi = km.astype(BF16)
    km_lo = (km - km_hi.astype(F32)).astype(BF16)
    gate = _dot(km_hi, qt) + _dot(km_lo, qt)
    first = Q_BLOCKS * t
    has_past = nb > Q_BLOCKS
    zeros = lambda r: jnp.zeros((r, Q_TILE), BF16)
    kaug_refs = (kaug0_ref, kaug1_ref)
    vt_refs = (vt0_ref, vt1_ref)

    kpos = lax.broadcasted_iota(jnp.int32, (blk, blk), 0)
    qpos = lax.broadcasted_iota(jnp.int32, (blk, blk), 1)
    causal = kpos <= qpos
    group = lambda j: slice(j * blk, (j + 1) * blk)

    def softmax_pv(s, m_ref, vts):
        p = jnp.exp2(s - m_ref).astype(BF16)
        w = s.shape[1] // len(vts)
        parts = [_dot(vt, p[:, j * w:(j + 1) * w]) for j, vt in enumerate(vts)]
        return parts[0] if len(parts) == 1 else jnp.concatenate(parts, axis=1)

    qt_own = (jnp.concatenate([qt[:hd], zeros(LANES - hd)], axis=0),
              jnp.concatenate([zeros(hd), qt[hd:]], axis=0))
    for head in range(2):
        for j in range(Q_BLOCKS):
            logits = _dot(kaug_refs[head][first + j], qt_own[head][:, group(j)])
            sa_ref[head, :, group(j)] = jnp.where(causal, logits, MASKED)

    lane = lax.broadcasted_iota(jnp.int32, (1, Q_TILE), 1)
    q_block = first + sum((lane >= j * blk).astype(jnp.int32) for j in range(1, Q_BLOCKS))
    mask0 = _moba_select(gate[:nb], q_block).astype(BF16)
    mask1 = _moba_select(gate[nb:], q_block).astype(BF16)
    qt_sel = (jnp.concatenate([qt[:hd], mask0, zeros(LANES - hd - nb)], axis=0),
              jnp.concatenate([mask1, zeros(hd - nb), qt[hd:]], axis=0))

    def chunks(skip):
        width = Q_TILE - skip * blk
        return [(a, min(a + PIPE_COLS, width)) for a in range(0, width, PIPE_COLS)]

    def issue_chunk(s_ref, n, head, skip, a, b_):
        q0 = skip * blk
        s_ref[head, :, a:b_] = _dot(kaug_refs[head][n], qt_sel[head][:, q0 + a:q0 + b_])

    def issue_head(s_ref, n, head, skip=0):
        for a, b_ in chunks(skip):
            issue_chunk(s_ref, n, head, skip, a, b_)

    def consume_chunk(s_ref, n, head, m, acc, skip, a, b_):
        q0 = skip * blk
        s = s_ref[head, :, a:b_]
        m_old = m[:, q0 + a:q0 + b_]
        m_new = jnp.maximum(m_old, jnp.max(s, axis=0, keepdims=True))
        return m_new, (acc[:, q0 + a:q0 + b_] * jnp.exp2(m_old - m_new)
                       + softmax_pv(s, m_new, [vt_refs[head][n]]))

    def step(dst_ref, n_issue, src_ref, n_consume, carry, issue_skip=0, consume_skip=0):
        out = ()
        for head in range(2):
            m, acc = carry[2 * head], carry[2 * head + 1]
            q0 = consume_skip * blk
            m_cols = [m[:, :q0]] if q0 else []
            acc_cols = [acc[:, :q0]] if q0 else []
            todo_issue = chunks(issue_skip) if n_issue is not None else []
            todo_consume = chunks(consume_skip)
            for k in range(max(len(todo_issue), len(todo_consume))):
                if k < len(todo_issue):
                    issue_chunk(dst_ref, n_issue, head, issue_skip, *todo_issue[k])
                if k < len(todo_consume):
                    m_new, acc_new = consume_chunk(src_ref, n_consume, head, m, acc, consume_skip,
                                                   *todo_consume[k])
                    m_cols.append(m_new)
                    acc_cols.append(acc_new)
            out += (jnp.concatenate(m_cols, axis=1), jnp.concatenate(acc_cols, axis=1))
        return out

    bufs = (sb_ref, sa_ref)
    for head in range(2):
        issue_head(sb_ref, first, head, skip=1)
    carry = ()
    for head in range(2):
        m_cols, acc_cols = [], []
        for a, b_ in chunks(0):
            s = sa_ref[head, :, a:b_]
            m = jnp.max(s, axis=0, keepdims=True)
            m_cols.append(m)
            acc_cols.append(softmax_pv(s, m, [vt_refs[head][first + j]
                                              for j in range(a // blk, b_ // blk)]))
        carry += (jnp.concatenate(m_cols, axis=1), jnp.concatenate(acc_cols, axis=1))
    for j in range(Q_BLOCKS - 1):
        if j + 1 < Q_BLOCKS - 1:
            carry = step(bufs[(j + 1) % 2], first + j + 1, bufs[j % 2], first + j, carry,
                         issue_skip=j + 2, consume_skip=j + 1)
        else:
            carry = step(bufs[(j + 1) % 2], 0 if has_past else None, bufs[j % 2], first + j, carry,
                         consume_skip=j + 1)

    def two_blocks(n, carry):
        carry = step(sb_ref, n + 1, sa_ref, n, carry)
        return step(sa_ref, n + 2, sb_ref, n + 1, carry)

    if has_past:
        carry = lax.fori_loop(
            0, first // 4, lambda j, c: two_blocks(4 * j + 2, two_blocks(4 * j, c)), carry)
        if Q_BLOCKS % 4:
            carry = lax.fori_loop(0, (first % 4) // 2, lambda _, c: two_blocks(first - 2, c), carry)
    _, a0, _, a1 = carry
    out_t = jnp.concatenate([a0[:hd] / a0[hd:hd + 1], a1[:hd] / a1[hd:hd + 1]], axis=0)
    o_ref[0] = out_t.T.astype(BF16)


def _moba(u):
    b, s, _ = u.shape
    nb = s // MOBA_BLOCK
    assert nb % ONES_ROWS == 0 and nb <= ATTN_HEAD_DIM and s % Q_TILE == 0 and Q_BLOCKS % 2 == 0
    pairs = ATTN_WIDTH // LANES
    return pl.pallas_call(
        functools.partial(_moba_kernel, nb=nb),
        grid=(b, pairs, s // Q_TILE),
        in_specs=[
            pl.BlockSpec((1, Q_TILE, LANES), lambda bi, hp, t: (bi, t, hp)),
            pl.BlockSpec((1, s, LANES), lambda bi, hp, t: (bi, 0, pairs + hp)),
            pl.BlockSpec((1, s, LANES), lambda bi, hp, t: (bi, 0, 2 * pairs + hp)),
        ],
        out_specs=pl.BlockSpec((1, Q_TILE, LANES), lambda bi, hp, t: (bi, t, hp)),
        out_shape=jax.ShapeDtypeStruct((b, s, ATTN_WIDTH), BF16),
        scratch_shapes=[
            pltpu.VMEM((nb, MOBA_BLOCK, LANES), BF16),
            pltpu.VMEM((nb, MOBA_BLOCK, LANES), BF16),
            pltpu.VMEM((nb, VT_ROWS, MOBA_BLOCK), BF16),
            pltpu.VMEM((nb, VT_ROWS, MOBA_BLOCK), BF16),
            pltpu.VMEM((2 * nb, LANES), F32),
            pltpu.VMEM((2, MOBA_BLOCK, Q_TILE), F32),
            pltpu.VMEM((2, MOBA_BLOCK, Q_TILE), F32),
        ],
        compiler_params=_params("parallel", "parallel", "arbitrary"),
        name="moba",
    )(u, u, u)


def _tail_kernel(*refs, final_norm, n_cast):
    (h_ref, a_ref, r_ref, p_ref, wo_ref, g_ffn_ref, w_in_ref, w_out_ref, g_ple_ref, wpg_ref,
     wpp_ref, g_final_ref) = refs[:12]
    cast_srcs = refs[12:12 + n_cast]
    o_ref = refs[12 + n_cast]
    cast_dsts = refs[13 + n_cast:13 + 2 * n_cast]
    act_ref = refs[13 + 2 * n_cast]
    _run_casts(cast_srcs, cast_dsts)
    halves = [slice(i * ROW_TILE // 2, (i + 1) * ROW_TILE // 2) for i in range(2)]
    h1 = [h_ref[rows, :] + _dot(a_ref[rows, :], wo_ref[:ATTN_WIDTH, :])
          + _dot(r_ref[rows, :], wo_ref[ATTN_WIDTH:, :]) for rows in halves]
    proj = [_dot(p_ref[rows, :].astype(BF16), wpp_ref[...]) for rows in halves]
    n1 = [_rms(x, g_ffn_ref[...]).astype(BF16) for x in h1]
    for n, rows in zip(n1, halves):
        for c in range(N_FF_CHUNKS):
            zg = _dot(n, w_in_ref[:, c * FF_CHUNK:(c + 1) * FF_CHUNK])
            zu = _dot(n, w_in_ref[:, D_FF + c * FF_CHUNK:D_FF + (c + 1) * FF_CHUNK])
            act_ref[rows, c * FF_CHUNK:(c + 1) * FF_CHUNK] = (zg * _sigmoid(zg) * zu).astype(BF16)
    h2 = [x + _dot(act_ref[rows, :], w_out_ref[...]) for x, rows in zip(h1, halves)]
    gate = [_sigmoid(_dot(_rms(x, g_ple_ref[...]).astype(BF16), wpg_ref[...])) for x in h2]
    for x, g, pr, rows in zip(h2, gate, proj, halves):
        out = x + g * pr
        if final_norm:
            out = _rms(out, g_final_ref[...])
        o_ref[rows, :] = out


def _tail(h, a, r, p, layer, wo, g_ffn, w_ffn_in, w_ffn_out, g_ple, wpg, wpp, g_final, final_norm,
          to_bf16):
    t = h.shape[0]
    steps = t // ROW_TILE
    casts = [_Cast(w_f32, cast_layer, steps) for w_f32, cast_layer in to_bf16]
    row = lambda w: pl.BlockSpec((ROW_TILE, w), lambda i: (i, 0))
    out, *converted = pl.pallas_call(
        functools.partial(_tail_kernel, final_norm=final_norm, n_cast=len(casts)),
        grid=(steps,),
        in_specs=[row(D_MODEL), row(ATTN_WIDTH), row(RET_WIDTH),
                  pl.BlockSpec((ROW_TILE, PLE_DIM), lambda i: (layer * steps + i, 0)),
                  _resident((ATTN_WIDTH + RET_WIDTH, D_MODEL)),
                  _resident((1, D_MODEL), layer), _resident((D_MODEL, 2 * D_FF)),
                  _resident((D_FF, D_MODEL)), _resident((1, D_MODEL), layer),
                  _resident((D_MODEL, D_MODEL)), _resident((PLE_DIM, D_MODEL)),
                  _resident((1, D_MODEL))] + [cast.in_spec for cast in casts],
        out_specs=[row(D_MODEL)] + [cast.out_spec for cast in casts],
        out_shape=[jax.ShapeDtypeStruct((t, D_MODEL), F32)] + [cast.out_shape for cast in casts],
        scratch_shapes=[pltpu.VMEM((ROW_TILE, D_FF), BF16)],
        compiler_params=_params("arbitrary"),
        name="tail",
    )(h, a, r, p, wo, g_ffn, w_ffn_in, w_ffn_out, g_ple, wpg, wpp, g_final,
      *[w_f32 for w_f32, _ in to_bf16])
    return out, converted


def kernel(x, p, attn_norm_g, w_in, ret_norm_g, w_out, ffn_norm_g, w_ffn_in, w_ffn_out,
           ple_norm_g, w_ple_gate, w_ple_proj, final_norm_g):
    b, s, d = x.shape
    depth = p.shape[0]
    assert d == D_MODEL and s % MOBA_BLOCK == 0 and s >= (MOBA_TOPK + 1) * MOBA_BLOCK
    assert (b * s) % ROW_TILE == 0 and s // MOBA_BLOCK <= LANES - ATTN_HEAD_DIM
    t = b * s
    h = x.reshape(t, d)
    gains = lambda g: g.reshape(depth, 1, -1).astype(F32)
    attn_g, ffn_g, ple_g = gains(attn_norm_g), gains(ffn_norm_g), gains(ple_norm_g)
    ret_g = ret_norm_g.reshape(depth, RET_HEADS, 1, RET_HEAD_DIM).astype(F32)
    p_rows = p.reshape(depth * t, PLE_DIM)
    tail_weights = [w_out, w_ffn_in, w_ffn_out, w_ple_gate, w_ple_proj]
    w_in_bf16 = w_in[0].astype(BF16)
    for i in range(depth):
        last = i == depth - 1
        u, r, tail_bf16 = _in_proj_retention(h, attn_g, w_in_bf16, ret_g, i, s, tail_weights)
        a = _moba(u.reshape(b, s, 3 * ATTN_WIDTH)).reshape(t, ATTN_WIDTH)
        wo, wf_in, wf_out, wpg, wpp = tail_bf16
        h, next_w_in = _tail(h, a, r, p_rows, i, wo, ffn_g, wf_in, wf_out, ple_g, wpg, wpp,
                             final_norm_g.reshape(1, -1).astype(F32), final_norm=last,
                             to_bf16=[] if last else [(w_in, i + 1)])
        if not last:
            (w_in_bf16,) = next_w_in
    return h.reshape(b, s, d)
```
